```python
import jax
import jax.numpy as jnp
from jax import lax
import numpy as np

D_MODEL = 1024
BATCH = 2
SEQ = 8192
DEPTH = 2
DEC_BATCH = 128
DEC_SEQ = 4
PAST_LEN = 16384
PAGE_SIZE = 128

N_MIXERS = 2
N_MLA_LAYERS = (DEPTH + 1) // 2
N_RWKV_LAYERS = DEPTH // 2
N_META = 16
D_FF = 4 * D_MODEL
RMS_EPS = 1e-6
NEG_INF = -1e30

MLA_HEADS = 8
QK_NOPE = 128
QK_ROPE = 64
V_HEAD = 128
Q_LORA = 3 * D_MODEL // 8
KV_LORA = D_MODEL // 4
ROPE_THETA = 10000.0
ATTN_SCALE = (QK_NOPE + QK_ROPE) ** -0.5
Q_BLOCK = 128

RWKV_HEAD = 64
RWKV_HEADS = D_MODEL // RWKV_HEAD
DECAY_LORA = 64
AAA_LORA = 64
GATE_LORA = 128
GN_EPS = 64e-5

kernel_name = 'mla_rwkv7_hybrid_step'


def rmsnorm(x, g):
    xf = x.astype(jnp.float32)
    y = xf * lax.rsqrt(jnp.mean(xf * xf, axis=-1, keepdims=True) + RMS_EPS)
    return (y * g.astype(jnp.float32)).astype(x.dtype)


def rope(x, pos):
    half = x.shape[-1] // 2
    inv_freq = ROPE_THETA ** (-jnp.arange(half, dtype=jnp.float32) / half)
    ang = pos.astype(jnp.float32)[:, None] * inv_freq[None, :]
    cos = jnp.cos(ang)[:, None, :]
    sin = jnp.sin(ang)[:, None, :]
    xf = x.astype(jnp.float32)
    x1, x2 = xf[..., :half], xf[..., half:]
    return jnp.concatenate([x1 * cos - x2 * sin, x2 * cos + x1 * sin], axis=-1).astype(x.dtype)


def mla_project(h, pos, w_qkv_a, q_a_norm, kv_a_norm, w_q_b, w_kv_b):
    B, T, _ = h.shape
    a = h @ w_qkv_a
    c_q = rmsnorm(a[..., :Q_LORA], q_a_norm)
    c_kv = rmsnorm(a[..., Q_LORA:Q_LORA + KV_LORA], kv_a_norm)
    k_pe = rope(a[..., Q_LORA + KV_LORA:][:, :, None, :], pos)[:, :, 0]
    q = (c_q @ w_q_b).reshape(B, T, MLA_HEADS, QK_NOPE + QK_ROPE)
    q_pe = rope(q[..., QK_NOPE:], pos)
    w_uk = w_kv_b.reshape(KV_LORA, MLA_HEADS, QK_NOPE + V_HEAD)[..., :QK_NOPE]
    q_lat = jnp.einsum('bthn,chn->bthc', q[..., :QK_NOPE], w_uk)
    return q_lat, q_pe, c_kv, k_pe


def mla_output(o_lat, w_kv_b, w_o):
    B, T = o_lat.shape[:2]
    w_uv = w_kv_b.reshape(KV_LORA, MLA_HEADS, QK_NOPE + V_HEAD)[..., QK_NOPE:]
    o = jnp.einsum('bthc,chv->bthv', o_lat, w_uv).reshape(B, T, MLA_HEADS * V_HEAD)
    return o @ w_o


def mla_prompt_attend(q_lat, q_pe, c_kv, k_pe):
    B, T = q_lat.shape[:2]
    n_blk = -(-T // Q_BLOCK)
    pad = n_blk * Q_BLOCK - T

    def blocks(t):
        t = jnp.pad(t, ((0, 0), (0, pad), (0, 0), (0, 0)))
        return jnp.moveaxis(t.reshape((B, n_blk, Q_BLOCK) + t.shape[2:]), 1, 0)

    key_pos = jnp.arange(T)

    def one_block(args):
        ql, qp, start = args
        s = jnp.einsum('bqhc,bsc->bhqs', ql, c_kv) + jnp.einsum('bqhr,bsr->bhqs', qp, k_pe)
        q_pos = start + jnp.arange(Q_BLOCK)
        s = jnp.where(key_pos[None, :] <= q_pos[:, None], s.astype(jnp.float32) * ATTN_SCALE, NEG_INF)
        p = jax.nn.softmax(s, axis=-1).astype(c_kv.dtype)
        return jnp.einsum('bhqs,bsc->bqhc', p, c_kv)

    o = lax.map(one_block, (blocks(q_lat), blocks(q_pe), jnp.arange(n_blk) * Q_BLOCK))
    o = jnp.moveaxis(o, 0, 1).reshape(B, n_blk * Q_BLOCK, MLA_HEADS, KV_LORA)
    return o[:, :T]


def mla_sample_attend(q_lat, q_pe, c_new, kpe_new, c_past, kpe_past):
    S = q_lat.shape[1]
    P = c_past.shape[1]
    s_past = jnp.einsum('bqhc,bpc->bhqp', q_lat, c_past) + jnp.einsum('bqhr,bpr->bhqp', q_pe, kpe_past)
    s_new = jnp.einsum('bqhc,bsc->bhqs', q_lat, c_new) + jnp.einsum('bqhr,bsr->bhqs', q_pe, kpe_new)
    causal = jnp.arange(S)[None, :] <= jnp.arange(S)[:, None]
    s_new = jnp.where(causal, s_new.astype(jnp.float32) * ATTN_SCALE, NEG_INF)
    s = jnp.concatenate([s_past.astype(jnp.float32) * ATTN_SCALE, s_new], axis=-1)
    p = jax.nn.softmax(s, axis=-1).astype(c_new.dtype)
    return (jnp.einsum('bhqp,bpc->bqhc', p[..., :P], c_past)
            + jnp.einsum('bhqs,bsc->bqhc', p[..., P:], c_new))


def _wkv_step(S, inp):
    r, w, k, v, a, b = inp
    sa = jnp.einsum('bhvk,bhk->bhv', S, a)
    S = S * w[:, :, None, :] + sa[..., None] * b[:, :, None, :] + v[..., None] * k[:, :, None, :]
    return S, jnp.einsum('bhvk,bhk->bhv', S, r)


def rwkv7_time_mix(h, shift_prev, wkv0, mu, w_r, w_k, w_v, w_o, dw0, dw1, dw2,
                   aw0, aw1, aw2, gw1, gw2, k_k, k_a, r_k, ln_w, ln_b):
    B, T, D = h.shape
    H, N = RWKV_HEADS, RWKV_HEAD
    x_prev = jnp.concatenate([shift_prev.astype(h.dtype)[:, None], h[:, :-1]], axis=1)
    xx = x_prev - h
    xr, xw, xk, xv, xa, xg = (h + xx * mu[j] for j in range(6))
    r = xr @ w_r
    k = xk @ w_k
    v = xv @ w_v
    log_w = -jax.nn.softplus(-(dw0 + jnp.tanh(xw @ dw1) @ dw2)) - 0.5
    a = jax.nn.sigmoid(aw0 + (xa @ aw1) @ aw2)
    g = jax.nn.sigmoid(xg @ gw1) @ gw2

    def heads(t):
        return t.astype(jnp.float32).reshape(B, T, H, N)

    kk = heads(k * k_k)
    kk = kk / jnp.maximum(jnp.sqrt(jnp.sum(kk * kk, axis=-1, keepdims=True)), 1e-12)
    k = heads(k * (1.0 + (a - 1.0) * k_a))
    r, v, a = heads(r), heads(v), heads(a)
    decay = jnp.exp(-jnp.exp(heads(log_w)))

    def tm(t):
        return jnp.moveaxis(t, 1, 0)

    S_T, y = lax.scan(_wkv_step, wkv0.astype(jnp.float32),
                      (tm(r), tm(decay), tm(k), tm(v), tm(-kk), tm(kk * a)))
    y = jnp.moveaxis(y, 0, 1)
    mean = jnp.mean(y, axis=-1, keepdims=True)
    var = jnp.mean(jnp.square(y - mean), axis=-1, keepdims=True)
    y = ((y - mean) * lax.rsqrt(var + GN_EPS)).reshape(B, T, D) * ln_w.astype(jnp.float32) + ln_b.astype(jnp.float32)
    y = y + (jnp.sum(r * k * r_k.astype(jnp.float32), axis=-1, keepdims=True) * v).reshape(B, T, D)
    out = (y.astype(h.dtype) * g) @ w_o
    return out, h[:, -1], S_T


def sq_relu_mlp(x, w_up, w_down):
    u = jax.nn.relu(x @ w_up)
    return (u * u) @ w_down


def setup_inputs(seed: int = 0) -> dict:
    key = jax.random.key(seed)
    ks = iter(jax.random.split(key, 48))
    f32 = jnp.float32

    def nrm(shape, scale=1.0):
        return jax.random.normal(next(ks), shape, f32) * scale

    def uni(shape, lo, hi):
        return jax.random.uniform(next(ks), shape, f32, lo, hi)

    n_pages = PAST_LEN // PAGE_SIZE
    pool_pages = (DEC_BATCH * n_pages * 5) // 4
    page_table = jax.random.permutation(next(ks), pool_pages)[:DEC_BATCH * n_pages]
    page_table = page_table.reshape(DEC_BATCH, n_pages).astype(jnp.int32)
    NA, NR, D = N_MLA_LAYERS, N_RWKV_LAYERS, D_MODEL
    H, N = RWKV_HEADS, RWKV_HEAD
    return dict(
        x_prompt=nrm((BATCH, SEQ, D)),
        x_sample=nrm((DEC_BATCH, DEC_SEQ, D)),
        cache_kv_latent=nrm((NA, pool_pages, PAGE_SIZE, KV_LORA)),
        cache_k_rope=nrm((NA, pool_pages, PAGE_SIZE, QK_ROPE)),
        state_wkv=nrm((NR, DEC_BATCH, H, N, N), 0.3),
        state_shift=nrm((NR, DEC_BATCH, D)),
        page_table=page_table,
        meta_tokens=nrm((N_META, D)),
        norm_mix=1.0 + nrm((DEPTH, D), 0.1),
        norm_ffn=1.0 + nrm((DEPTH, D), 0.1),
        norm_final=1.0 + nrm((D,), 0.1),
        mla_w_qkv_a=nrm((NA, D, Q_LORA + KV_LORA + QK_ROPE), D ** -0.5),
        mla_q_a_norm=1.0 + nrm((NA, Q_LORA), 0.1),
        mla_kv_a_norm=1.0 + nrm((NA, KV_LORA), 0.1),
        mla_w_q_b=nrm((NA, Q_LORA, MLA_HEADS * (QK_NOPE + QK_ROPE)), Q_LORA ** -0.5),
        mla_w_kv_b=nrm((NA, KV_LORA, MLA_HEADS * (QK_NOPE + V_HEAD)), KV_LORA ** -0.5),
        mla_w_o=nrm((NA, MLA_HEADS * V_HEAD, D), (MLA_HEADS * V_HEAD) ** -0.5),
        rwkv_mu=uni((NR, 6, D), 0.0, 1.0),
        rwkv_w_r=nrm((NR, D, D), D ** -0.5),
        rwkv_w_k=nrm((NR, D, D), D ** -0.5),
        rwkv_w_v=nrm((NR, D, D), D ** -0.5),
        rwkv_w_o=nrm((NR, D, D), D ** -0.5),
        rwkv_decay_w0=uni((NR, D), -6.0, 1.0),
        rwkv_decay_w1=nrm((NR, D, DECAY_LORA), D ** -0.5),
        rwkv_decay_w2=nrm((NR, DECAY_LORA, D), 0.5 * DECAY_LORA ** -0.5),
        rwkv_a_w0=nrm((NR, D), 0.5),
        rwkv_a_w1=nrm((NR, D, AAA_LORA), D ** -0.5),
        rwkv_a_w2=nrm((NR, AAA_LORA, D), 0.5 * AAA_LORA ** -0.5),
        rwkv_g_w1=nrm((NR, D, GATE_LORA), D ** -0.5),
        rwkv_g_w2=nrm((NR, GATE_LORA, D), GATE_LORA ** -0.5),
        rwkv_k_k=0.85 + nrm((NR, D), 0.1),
        rwkv_k_a=1.0 + nrm((NR, D), 0.1),
        rwkv_r_k=nrm((NR, H, N), 0.1),
        rwkv_ln_w=1.0 + nrm((NR, D), 0.1),
        rwkv_ln_b=nrm((NR, D), 0.02),
        ffn_w_up=nrm((DEPTH, D, D_FF), D ** -0.5),
        ffn_w_down=nrm((DEPTH, D_FF, D), D_FF ** -0.5),
    )


def reference(x_prompt, x_sample, cache_kv_latent, cache_k_rope, state_wkv, state_shift, page_table,
              meta_tokens, norm_mix, norm_ffn, norm_final,
              mla_w_qkv_a, mla_q_a_norm, mla_kv_a_norm, mla_w_q_b, mla_w_kv_b, mla_w_o,
              rwkv_mu, rwkv_w_r, rwkv_w_k, rwkv_w_v, rwkv_w_o,
              rwkv_decay_w0, rwkv_decay_w1, rwkv_decay_w2,
              rwkv_a_w0, rwkv_a_w1, rwkv_a_w2, rwkv_g_w1, rwkv_g_w2,
              rwkv_k_k, rwkv_k_a, rwkv_r_k, rwkv_ln_w, rwkv_ln_b,
              ffn_w_up, ffn_w_down):
    f32 = jnp.float32
    B = x_prompt.shape[0]
    DB = x_sample.shape[0]
    meta = jnp.broadcast_to(meta_tokens.astype(x_prompt.dtype)[None], (B, N_META, D_MODEL))
    hp = jnp.concatenate([meta, x_prompt], axis=1)
    hs = x_sample
    pos_p = jnp.arange(hp.shape[1])
    pos_s = PAST_LEN + jnp.arange(hs.shape[1])
    kv_lat_p, k_rope_p, kv_lat_s, k_rope_s = [], [], [], []
    wkv_p, shift_p, wkv_s, shift_s = [], [], [], []
    for i in range(DEPTH):
        n_p = rmsnorm(hp, norm_mix[i])
        n_s = rmsnorm(hs, norm_mix[i])
        l = i // N_MIXERS
        if i % N_MIXERS == 0:
            proj = (mla_w_qkv_a[l], mla_q_a_norm[l], mla_kv_a_norm[l], mla_w_q_b[l], mla_w_kv_b[l])
            ql, qp, c_p, kp_p = mla_project(n_p, pos_p, *proj)
            y_p = mla_output(mla_prompt_attend(ql, qp, c_p, kp_p), mla_w_kv_b[l], mla_w_o[l])
            ql, qp, c_s, kp_s = mla_project(n_s, pos_s, *proj)
            c_past = cache_kv_latent[l, page_table].reshape(DB, -1, KV_LORA)
            kp_past = cache_k_rope[l, page_table].reshape(DB, -1, QK_ROPE)
            y_s = mla_output(mla_sample_attend(ql, qp, c_s, kp_s, c_past, kp_past), mla_w_kv_b[l], mla_w_o[l])
            kv_lat_p.append(c_p)
            k_rope_p.append(kp_p)
            kv_lat_s.append(c_s)
            k_rope_s.append(kp_s)
        else:
            rw = (rwkv_mu[l], rwkv_w_r[l], rwkv_w_k[l], rwkv_w_v[l], rwkv_w_o[l],
                  rwkv_decay_w0[l], rwkv_decay_w1[l], rwkv_decay_w2[l],
                  rwkv_a_w0[l], rwkv_a_w1[l], rwkv_a_w2[l], rwkv_g_w1[l], rwkv_g_w2[l],
                  rwkv_k_k[l], rwkv_k_a[l], rwkv_r_k[l], rwkv_ln_w[l], rwkv_ln_b[l])
            zero_shift = jnp.zeros((B, D_MODEL), f32)
            zero_state = jnp.zeros((B, RWKV_HEADS, RWKV_HEAD, RWKV_HEAD), f32)
            y_p, sh_p, st_p = rwkv7_time_mix(n_p, zero_shift, zero_state, *rw)
            y_s, sh_s, st_s = rwkv7_time_mix(n_s, state_shift[l], state_wkv[l], *rw)
            wkv_p.append(st_p)
            shift_p.append(sh_p)
            wkv_s.append(st_s)
            shift_s.append(sh_s)
        hp = hp + y_p
        hs = hs + y_s
        hp = hp + sq_relu_mlp(rmsnorm(hp, norm_ffn[i]), ffn_w_up[i], ffn_w_down[i])
        hs = hs + sq_relu_mlp(rmsnorm(hs, norm_ffn[i]), ffn_w_up[i], ffn_w_down[i])
    y_prompt = rmsnorm(hp, norm_final)[:, N_META:]
    y_sample = rmsnorm(hs, norm_final)
    return (y_prompt, y_sample,
            jnp.stack(kv_lat_p), jnp.stack(k_rope_p), jnp.stack(kv_lat_s), jnp.stack(k_rope_s),
            jnp.stack(wkv_p), jnp.stack(shift_p), jnp.stack(wkv_s), jnp.stack(shift_s))
```

```python
import functools
import math

import jax
import jax.numpy as jnp
from jax import lax
from jax.experimental import pallas as pl
from jax.experimental.pallas import tpu as pltpu

F32 = jnp.float32
BF16 = jnp.bfloat16

RMS_EPS = 1e-6
GN_EPS = 64e-5
NEG_INF = -1e30
ROPE_THETA = 10000.0
N_META = 16

LANES = 128
VMEM_LIMIT_BYTES = 56 * 1024 * 1024

ATTN_Q_TOKENS = 128
ATTN_K_CHUNK = 256
SEQ_ALIGN = 256
ROW_TILE = 768
RWKV_ROW_TILE = 384
FF_TILE = 1024
WKV_CHUNK = 64
PAGES_PER_STEP = 16


def _cparams(*sem):
    return pltpu.CompilerParams(dimension_semantics=sem, vmem_limit_bytes=VMEM_LIMIT_BYTES)


def _dot(a, b, **kw):
    return jnp.dot(a, b, preferred_element_type=F32, **kw)


def _dot_nt(a, b):
    return lax.dot_general(a, b, (((1,), (1,)), ((), ())), preferred_element_type=F32)


def _dot_tn(a, b, **kw):
    return lax.dot_general(a, b, (((0,), (0,)), ((), ())), preferred_element_type=F32, **kw)


def _rms(x, g):
    return x * lax.rsqrt(jnp.mean(x * x, axis=-1, keepdims=True) + RMS_EPS) * g


def _row_tile(n, target=ROW_TILE):
    best = None
    for t in range(8, min(n, target) + 1, 8):
        if n % t == 0:
            best = t
    assert best is not None, n
    return best


def _mla_proj_kernel(x_ref, g_ref, wa_ref, qn_ref, kvn_ref, wnope_ref, wpe_ref, wpesw_ref, wuk_ref,
                     cosq_ref, sinq_ref, cosk_ref, sink_ref,
                     qlat_ref, qpe_ref, ckv_ref, kpe_ref, ckvb_ref, kpeb_ref,
                     *, q_lora, kv_lora, rope, heads, nope, scale):
    n = _rms(x_ref[0], g_ref[...]).astype(BF16)
    a = _dot(n, wa_ref[...])
    c_q = _rms(a[:, :q_lora], qn_ref[...]).astype(BF16)
    c_kv = _rms(a[:, q_lora:q_lora + kv_lora], kvn_ref[...])
    o = q_lora + kv_lora
    k_pe = a[:, o:o + rope] * cosk_ref[...] + a[:, o + rope:o + 2 * rope] * sink_ref[...]
    ckv_ref[0] = c_kv
    kpe_ref[0] = k_pe
    ckvb_ref[0] = c_kv.astype(BF16)
    kpeb_ref[0] = k_pe.astype(BF16)
    q_pe = _dot(c_q, wpe_ref[...]) * cosq_ref[...] + _dot(c_q, wpesw_ref[...]) * sinq_ref[...]
    qpe_ref[0] = (q_pe * scale).astype(BF16)
    q_nope = _dot(c_q, wnope_ref[...]).astype(BF16)
    for h in range(heads):
        q_lat = _dot(q_nope[:, h * nope:(h + 1) * nope], wuk_ref[h])
        qlat_ref[0, :, h * kv_lora:(h + 1) * kv_lora] = (q_lat * scale).astype(BF16)


def _mla_project(x, g, wts, tabs, dims):
    B, T, D = x.shape
    q_lora, kv_lora, rope, heads, nope, scale = dims
    tm = _row_tile(T)
    wa, qn, kvn, wnope, wpe, wpesw, wuk = wts
    cosq, sinq, cosk, sink = tabs
    full = lambda arr: pl.BlockSpec(arr.shape, lambda b, i: (0,) * arr.ndim)
    row = lambda w: pl.BlockSpec((1, tm, w), lambda b, i: (b, i, 0))
    tab = lambda w: pl.BlockSpec((tm, w), lambda b, i: (i, 0))
    kern = functools.partial(_mla_proj_kernel, q_lora=q_lora, kv_lora=kv_lora, rope=rope,
                             heads=heads, nope=nope, scale=scale)
    return pl.pallas_call(
        kern,
        grid=(B, T // tm),
        in_specs=[row(D), full(g), full(wa), full(qn), full(kvn), full(wnope), full(wpe), full(wpesw),
                  full(wuk), tab(heads * rope), tab(heads * rope), tab(rope), tab(rope)],
        out_specs=[row(heads * kv_lora), row(heads * rope), row(kv_lora), row(rope), row(kv_lora), row(rope)],
        out_shape=[jax.ShapeDtypeStruct((B, T, heads * kv_lora), BF16),
                   jax.ShapeDtypeStruct((B, T, heads * rope), BF16),
                   jax.ShapeDtypeStruct((B, T, kv_lora), F32),
                   jax.ShapeDtypeStruct((B, T, rope), F32),
                   jax.ShapeDtypeStruct((B, T, kv_lora), BF16),
                   jax.ShapeDtypeStruct((B, T, rope), BF16)],
        compiler_params=_cparams("parallel", "parallel"),
        name="mla_project",
    )(x, g, wa, qn, kvn, wnope, wpe, wpesw, wuk, cosq, sinq, cosk, sink)


def _attn_prompt_kernel(ql_ref, qp_ref, c_ref, kp_ref, o_ref, m_scr, l_scr, acc_scr,
                        *, heads, tq, tk, kv_lora):
    qi = pl.program_id(1)
    rows = tq * heads
    m_scr[...] = jnp.full(m_scr.shape, -jnp.inf, F32)
    l_scr[...] = jnp.zeros(l_scr.shape, F32)
    acc_scr[...] = jnp.zeros(acc_scr.shape, F32)
    ql = ql_ref[0]
    qp = qp_ref[0]

    def chunk(j, masked):
        start = pl.multiple_of(j * tk, tk)
        kc = c_ref[0, pl.ds(start, tk), :]
        kp = kp_ref[0, pl.ds(start, tk), :]
        s = _dot_nt(ql, kc) + _dot_nt(qp, kp)
        if masked:
            tok = qi * tq + lax.broadcasted_iota(jnp.int32, (rows, tk), 0) // heads
            key = start + lax.broadcasted_iota(jnp.int32, (rows, tk), 1)
            s = jnp.where(key <= tok, s, NEG_INF)
        m_prev = m_scr[...]
        m_next = jnp.maximum(m_prev, jnp.max(s, axis=1, keepdims=True))
        p = jnp.exp(s - jnp.tile(m_next, (1, tk // LANES)))
        alpha = jnp.exp(m_prev - m_next)
        l_scr[...] = alpha * l_scr[...] + jnp.sum(p, axis=1, keepdims=True)
        acc_scr[...] = acc_scr[...] * jnp.tile(alpha, (1, kv_lora // LANES)) + _dot(p.astype(BF16), kc)
        m_scr[...] = m_next

    n_full = (qi * tq + 1) // tk
    n_total = (qi * tq + tq - 1) // tk + 1

    def full_body(j, carry):
        chunk(j, False)
        return carry

    def masked_body(j, carry):
        chunk(j, True)
        return carry

    lax.fori_loop(0, n_full, full_body, 0)
    lax.fori_loop(n_full, n_total, masked_body, 0)
    inv = 1.0 / l_scr[...]
    o_ref[0] = (acc_scr[...] * jnp.tile(inv, (1, kv_lora // LANES))).astype(o_ref.dtype)


def _attn_prompt(q_lat, q_pe, c_kv, k_pe, heads):
    B, TH, C = q_lat.shape
    R = q_pe.shape[-1]
    T = c_kv.shape[1]
    tq, tk = ATTN_Q_TOKENS, ATTN_K_CHUNK
    assert T % tk == 0 and tk % tq == 0 and C % LANES == 0
    rows = tq * heads
    kern = functools.partial(_attn_prompt_kernel, heads=heads, tq=tq, tk=tk, kv_lora=C)
    return pl.pallas_call(
        kern,
        grid=(B, T // tq),
        in_specs=[pl.BlockSpec((1, rows, C), lambda b, i: (b, i, 0)),
                  pl.BlockSpec((1, rows, R), lambda b, i: (b, i, 0)),
                  pl.BlockSpec((1, T, C), lambda b, i: (b, 0, 0)),
                  pl.BlockSpec((1, T, R), lambda b, i: (b, 0, 0))],
        out_specs=pl.BlockSpec((1, rows, C), lambda b, i: (b, i, 0)),
        out_shape=jax.ShapeDtypeStruct((B, TH, C), BF16),
        scratch_shapes=[pltpu.VMEM((rows, LANES), F32), pltpu.VMEM((rows, LANES), F32),
                        pltpu.VMEM((rows, C), F32)],
        compiler_params=_cparams("parallel", "arbitrary"),
        name="attn_prompt",
    )(q_lat, q_pe, c_kv, k_pe)


def _attn_sample_kernel(pt_ref, ql_ref, qp_ref, cn_ref, kn_ref, *rest, heads, pages, page, kv_lora):
    c_refs = rest[:pages]
    k_refs = rest[pages:2 * pages]
    o_ref = rest[2 * pages]
    m_scr, l_scr, acc_scr = rest[2 * pages + 1:]
    j = pl.program_id(1)
    rows = ql_ref.shape[1]

    @pl.when(j == 0)
    def _():
        m_scr[...] = jnp.full(m_scr.shape, -jnp.inf, F32)
        l_scr[...] = jnp.zeros(l_scr.shape, F32)
        acc_scr[...] = jnp.zeros(acc_scr.shape, F32)

    ql = ql_ref[0]
    qp = qp_ref[0]

    def update(s, vals):
        m_prev = m_scr[...]
        m_next = jnp.maximum(m_prev, jnp.max(s, axis=1, keepdims=True))
        p = jnp.exp(s - m_next[:, :1])
        alpha = jnp.exp(m_prev - m_next)
        l_scr[...] = alpha * l_scr[...] + jnp.sum(p, axis=1, keepdims=True)
        acc_scr[...] = acc_scr[...] * jnp.tile(alpha, (1, kv_lora // LANES)) + _dot(p.astype(BF16), vals)
        m_scr[...] = m_next

    kc = jnp.concatenate([r[...].astype(BF16) for r in c_refs], axis=0)
    kp = jnp.concatenate([r[...].astype(BF16) for r in k_refs], axis=0)
    update(_dot_nt(ql, kc) + _dot_nt(qp, kp), kc)

    @pl.when(j == pl.num_programs(1) - 1)
    def _():
        cn = cn_ref[0]
        s = _dot_nt(ql, cn) + _dot_nt(qp, kn_ref[0])
        n_new = cn.shape[0]
        tok = lax.broadcasted_iota(jnp.int32, (rows, n_new), 0) // heads
        key = lax.broadcasted_iota(jnp.int32, (rows, n_new), 1)
        update(jnp.where(key <= tok, s, NEG_INF), cn)
        inv = 1.0 / l_scr[...]
        o_ref[0] = (acc_scr[...] * jnp.tile(inv, (1, kv_lora // LANES))).astype(o_ref.dtype)


def _attn_sample(q_lat, q_pe, c_new, k_new, cache_c, cache_k, page_table, layer, heads):
    DB, rows, C = q_lat.shape
    R = q_pe.shape[-1]
    S = c_new.shape[1]
    n_pages = page_table.shape[1]
    page = cache_c.shape[2]
    G = math.gcd(PAGES_PER_STEP, n_pages)
    steps = n_pages // G
    pt = page_table.reshape(-1).astype(jnp.int32)

    def cache_spec(width, g):
        return pl.BlockSpec((None, None, page, width),
                            lambda b, j, pt_ref: (layer, pt_ref[b * n_pages + j * G + g], 0, 0))

    per_b = lambda shape: pl.BlockSpec((1,) + shape, lambda b, j, pt_ref: (b, 0, 0))
    kern = functools.partial(_attn_sample_kernel, heads=heads, pages=G, page=page, kv_lora=C)
    grid_spec = pltpu.PrefetchScalarGridSpec(
        num_scalar_prefetch=1,
        grid=(DB, steps),
        in_specs=[per_b((rows, C)), per_b((rows, R)), per_b((S, C)), per_b((S, R))]
        + [cache_spec(C, g) for g in range(G)] + [cache_spec(R, g) for g in range(G)],
        out_specs=per_b((rows, C)),
        scratch_shapes=[pltpu.VMEM((rows, LANES), F32), pltpu.VMEM((rows, LANES), F32),
                        pltpu.VMEM((rows, C), F32)],
    )
    return pl.pallas_call(
        kern,
        grid_spec=grid_spec,
        out_shape=jax.ShapeDtypeStruct((DB, rows, C), BF16),
        compiler_params=_cparams("parallel", "arbitrary"),
        name="attn_sample",
    )(pt, q_lat, q_pe, c_new, k_new, *([cache_c] * G), *([cache_k] * G))


def _mla_out_kernel(o_ref, x_ref, wuv_ref, wo_ref, h_ref, cat_scr, *, heads, kv_lora, vhead):
    for h in range(heads):
        oh = _dot(o_ref[:, h * kv_lora:(h + 1) * kv_lora], wuv_ref[h])
        cat_scr[:, h * vhead:(h + 1) * vhead] = oh.astype(BF16)
    h_ref[...] = x_ref[...] + _dot(cat_scr[...], wo_ref[...])


def _mla_out(o_lat, x, wuv, wo):
    N, D = x.shape
    heads, kv_lora, vhead = wuv.shape
    tm = _row_tile(N)
    kern = functools.partial(_mla_out_kernel, heads=heads, kv_lora=kv_lora, vhead=vhead)
    return pl.pallas_call(
        kern,
        grid=(N // tm,),
        in_specs=[pl.BlockSpec((tm, heads * kv_lora), lambda i: (i, 0)),
                  pl.BlockSpec((tm, D), lambda i: (i, 0)),
                  pl.BlockSpec(wuv.shape, lambda i: (0, 0, 0)),
                  pl.BlockSpec(wo.shape, lambda i: (0, 0))],
        out_specs=pl.BlockSpec((tm, D), lambda i: (i, 0)),
        out_shape=jax.ShapeDtypeStruct((N, D), F32),
        scratch_shapes=[pltpu.VMEM((tm, heads * vhead), BF16)],
        compiler_params=_cparams("parallel"),
        name="mla_out",
    )(o_lat, x, wuv, wo)


def _ffn_kernel(h_ref, g_ref, wup_ref, wdn_ref, gnext_ref, hout_ref, nnext_ref, xn_scr, acc_scr):
    k = pl.program_id(1)

    @pl.when(k == 0)
    def _():
        xn_scr[...] = _rms(h_ref[...], g_ref[...]).astype(BF16)
        acc_scr[...] = jnp.zeros(acc_scr.shape, F32)

    u = jnp.maximum(_dot(xn_scr[...], wup_ref[...]), 0.0)
    acc_scr[...] += _dot((u * u).astype(BF16), wdn_ref[...])

    @pl.when(k == pl.num_programs(1) - 1)
    def _():
        out = h_ref[...] + acc_scr[...]
        hout_ref[...] = out
        nnext_ref[...] = _rms(out, gnext_ref[...])


def _ffn(h, g, wup, wdn, g_next):
    N, D = h.shape
    FF = wup.shape[1]
    tm = _row_tile(N)
    tf = min(FF_TILE, FF)
    return pl.pallas_call(
        _ffn_kernel,
        grid=(N // tm, FF // tf),
        in_specs=[pl.BlockSpec((tm, D), lambda i, k: (i, 0)),
                  pl.BlockSpec((1, D), lambda i, k: (0, 0)),
                  pl.BlockSpec((D, tf), lambda i, k: (0, k)),
                  pl.BlockSpec((tf, D), lambda i, k: (k, 0)),
                  pl.BlockSpec((1, D), lambda i, k: (0, 0))],
        out_specs=[pl.BlockSpec((tm, D), lambda i, k: (i, 0)),
                   pl.BlockSpec((tm, D), lambda i, k: (i, 0))],
        out_shape=[jax.ShapeDtypeStruct((N, D), F32), jax.ShapeDtypeStruct((N, D), F32)],
        scratch_shapes=[pltpu.VMEM((tm, D), BF16), pltpu.VMEM((tm, D), F32)],
        compiler_params=_cparams("parallel", "arbitrary"),
        name="ffn",
    )(h, g, wup, wdn, g_next)


def _rwkv_proj_kernel(n_ref, xp_ref, mu_ref, wr_ref, wk_ref, wv_ref, dw0_ref, dw1_ref, dw2_ref,
                      aw0_ref, aw1_ref, aw2_ref, gw1_ref, gw2_ref,
                      r_ref, k_ref, v_ref, d_ref, a_ref, g_ref):
    n = n_ref[...]
    xx = xp_ref[...] - n
    mix = lambda j: (n + xx * mu_ref[j:j + 1, :]).astype(BF16)
    r_ref[...] = _dot(mix(0), wr_ref[...])
    k_ref[...] = _dot(mix(2), wk_ref[...])
    v_ref[...] = _dot(mix(3), wv_ref[...])
    z = dw0_ref[...] + _dot(jnp.tanh(_dot(mix(1), dw1_ref[...])).astype(BF16), dw2_ref[...])
    d_ref[...] = (-math.exp(-0.5)) / (1.0 + jnp.exp(-z))
    za = aw0_ref[...] + _dot(_dot(mix(4), aw1_ref[...]).astype(BF16), aw2_ref[...])
    a_ref[...] = 1.0 / (1.0 + jnp.exp(-za))
    zg = _dot(mix(5), gw1_ref[...])
    g_ref[...] = _dot((1.0 / (1.0 + jnp.exp(-zg))).astype(BF16), gw2_ref[...])


def _rwkv_project(n, x_prev, wts):
    N, D = n.shape
    tm = _row_tile(N, RWKV_ROW_TILE)
    row = pl.BlockSpec((tm, D), lambda i: (i, 0))
    full = lambda arr: pl.BlockSpec(arr.shape, lambda i: (0,) * arr.ndim)
    return pl.pallas_call(
        _rwkv_proj_kernel,
        grid=(N // tm,),
        in_specs=[row, row] + [full(w) for w in wts],
        out_specs=[row] * 6,
        out_shape=[jax.ShapeDtypeStruct((N, D), F32)] * 6,
        compiler_params=_cparams("parallel"),
        name="rwkv_project",
    )(n, x_prev, *wts)


def _wkv_kernel(r_ref, k_ref, v_ref, d_ref, a_ref, kk_ref, ka_ref, rk_ref, lnw_ref, lnb_ref, s0_ref,
                y_ref, sout_ref, st_scr, *, L, t_valid, pairs, hd):
    c = pl.program_id(1)
    L2 = 2 * L
    hi = lax.Precision.HIGHEST
    m0 = lax.broadcasted_iota(jnp.int32, (1, 2 * hd), 1) < hd

    @pl.when(c == 0)
    def _():
        z = jnp.zeros((hd, hd), F32)
        for p in range(pairs):
            s_bd = jnp.concatenate([jnp.concatenate([s0_ref[0, 2 * p], z], axis=1),
                                    jnp.concatenate([z, s0_ref[0, 2 * p + 1]], axis=1)], axis=0)
            st_scr[p] = s_bd.T

    valid = (c * L + lax.broadcasted_iota(jnp.int32, (L, 1), 0)) < t_valid
    tri = (lax.broadcasted_iota(jnp.int32, (L, L), 0) >= lax.broadcasted_iota(jnp.int32, (L, L), 1)).astype(F32)
    i2 = lax.broadcasted_iota(jnp.int32, (L2, L2), 0)
    j2 = lax.broadcasted_iota(jnp.int32, (L2, L2), 1)
    mask_s = j2 < i2
    mask_i = j2 <= i2
    eye = (i2 == j2).astype(F32)
    ones_l = jnp.ones((L, 2 * hd), F32)

    def stack(x):
        return jnp.concatenate([jnp.where(m0, x, 0.0), jnp.where(m0, 0.0, x)], axis=0).astype(BF16)

    def head_sum(x):
        s_a = jnp.sum(jnp.where(m0, x, 0.0), axis=1, keepdims=True)
        s_b = jnp.sum(jnp.where(m0, 0.0, x), axis=1, keepdims=True)
        return jnp.where(m0, s_a, s_b)

    for p in range(pairs):
        sl = slice(p * 2 * hd, (p + 1) * 2 * hd)
        r = jnp.where(valid, r_ref[0, :, sl], 0.0)
        k = jnp.where(valid, k_ref[0, :, sl], 0.0)
        v = jnp.where(valid, v_ref[0, :, sl], 0.0)
        d = jnp.where(valid, d_ref[0, :, sl], 0.0)
        a = jnp.where(valid, a_ref[0, :, sl], 0.0)
        kk = k * kk_ref[:, sl]
        kk = kk / jnp.maximum(jnp.sqrt(head_sum(kk * kk)), 1e-12)
        kp = k * (1.0 + (a - 1.0) * ka_ref[:, sl])
        bv = kk * a
        cum = _dot(tri, d, precision=hi)
        c_last = cum[L - 1:L, :]
        e_neg = jnp.exp(-cum)
        e_rest = jnp.exp(c_last - cum)
        a_st = stack(-kk * jnp.exp(cum - d))
        r_st = stack(r * jnp.exp(cum))
        b_st = stack(bv * e_neg)
        k_st = stack(kp * e_neg)
        v_st = stack(v)
        if L2 % LANES == 0:
            pm = _dot_nt(jnp.concatenate([a_st, r_st], axis=0), jnp.concatenate([b_st, k_st], axis=0))
            m_ab, m_ak = pm[:L2, :L2], pm[:L2, L2:]
            m_rb, m_rk = pm[L2:, :L2], pm[L2:, L2:]
        else:
            m_ab, m_ak = _dot_nt(a_st, b_st), _dot_nt(a_st, k_st)
            m_rb, m_rk = _dot_nt(r_st, b_st), _dot_nt(r_st, k_st)
        m_ab = jnp.where(mask_s, m_ab, 0.0)
        m_ak = jnp.where(mask_s, m_ak, 0.0).astype(BF16)
        m_rb = jnp.where(mask_i, m_rb, 0.0).astype(BF16)
        m_rk = jnp.where(mask_i, m_rk, 0.0).astype(BF16)
        inv = eye + m_ab
        pw = m_ab
        for _ in range(int(math.log2(L)) - 1):
            pwb = pw.astype(BF16)
            pw = _dot(pwb, pwb)
            inv = inv + _dot(pw.astype(BF16), inv.astype(BF16))
        st = st_scr[p]
        stb = st.astype(BF16)
        rhs = _dot(jnp.concatenate([a_st, m_ak], axis=1), jnp.concatenate([stb, v_st], axis=0))
        u_st = _dot(inv.astype(BF16), rhs.astype(BF16)).astype(BF16)
        y_st = _dot(jnp.concatenate([r_st, m_rb, m_rk], axis=1), jnp.concatenate([stb, u_st, v_st], axis=0))
        y = y_st[:L] + y_st[L:]
        decay_col = jnp.exp(_dot_tn(d, ones_l, precision=hi))
        st_scr[p] = decay_col * st + _dot_tn(
            jnp.concatenate([stack(bv * e_rest), stack(kp * e_rest)], axis=0),
            jnp.concatenate([u_st, v_st], axis=0))
        mean = head_sum(y) * (1.0 / hd)
        yc = y - mean
        var = head_sum(yc * yc) * (1.0 / hd)
        yn = yc * lax.rsqrt(var + GN_EPS) * lnw_ref[:, sl] + lnb_ref[:, sl]
        y_ref[0, :, sl] = yn + head_sum(r * kp * rk_ref[:, sl]) * v

    @pl.when(c == pl.num_programs(1) - 1)
    def _():
        for p in range(pairs):
            s_bd = st_scr[p].T
            sout_ref[0, 2 * p] = s_bd[:hd, :hd]
            sout_ref[0, 2 * p + 1] = s_bd[hd:, hd:]


def _wkv(r, k, v, d, a, params, s0, L, t_valid):
    B, T, D = r.shape
    H, hd = s0.shape[1], s0.shape[2]
    assert 2 * hd == LANES and H % 2 == 0 and T % L == 0
    pairs = H // 2
    seq = pl.BlockSpec((1, L, D), lambda b, c: (b, c, 0))
    vec = pl.BlockSpec((1, D), lambda b, c: (0, 0))
    state = pl.BlockSpec((1, H, hd, hd), lambda b, c: (b, 0, 0, 0))
    kern = functools.partial(_wkv_kernel, L=L, t_valid=t_valid, pairs=pairs, hd=hd)
    return pl.pallas_call(
        kern,
        grid=(B, T // L),
        in_specs=[seq] * 5 + [vec] * 5 + [state],
        out_specs=[seq, state],
        out_shape=[jax.ShapeDtypeStruct((B, T, D), F32), jax.ShapeDtypeStruct(s0.shape, F32)],
        scratch_shapes=[pltpu.VMEM((pairs, LANES, LANES), F32)],
        compiler_params=_cparams("parallel", "arbitrary"),
        name="wkv",
    )(r, k, v, d, a, *params, s0)


def _gated_out_kernel(y_ref, g_ref, h_ref, wo_ref, o_ref):
    o_ref[...] = h_ref[...] + _dot((y_ref[...] * g_ref[...]).astype(BF16), wo_ref[...])


def _gated_out(y, g, h, wo):
    N, D = h.shape
    tm = _row_tile(N)
    row = pl.BlockSpec((tm, D), lambda i: (i, 0))
    return pl.pallas_call(
        _gated_out_kernel,
        grid=(N // tm,),
        in_specs=[row, row, row, pl.BlockSpec(wo.shape, lambda i: (0, 0))],
        out_specs=row,
        out_shape=jax.ShapeDtypeStruct((N, D), F32),
        compiler_params=_cparams("parallel"),
        name="rwkv_out",
    )(y, g, h, wo)


def _rope_tables(pos, rope, heads):
    half = rope // 2
    inv_freq = ROPE_THETA ** (-jnp.arange(half, dtype=F32) / half)
    ang = pos.astype(F32)[:, None] * inv_freq[None, :]
    cos, sin = jnp.cos(ang), jnp.sin(ang)
    cos2 = jnp.concatenate([cos, cos], axis=-1)
    sin2 = jnp.concatenate([-sin, sin], axis=-1)
    return jnp.tile(cos2, (1, heads)), jnp.tile(sin2, (1, heads)), cos2, sin2


def _swap_halves(w, width):
    lead = w.shape[:-1]
    g = w.reshape(lead + (-1, 2, width // 2))
    return g[..., ::-1, :].reshape(w.shape)


def kernel(x_prompt, x_sample, cache_kv_latent, cache_k_rope, state_wkv, state_shift, page_table, meta_tokens, norm_mix, norm_ffn, norm_final, mla_w_qkv_a, mla_q_a_norm, mla_kv_a_norm, mla_w_q_b, mla_w_kv_b, mla_w_o, rwkv_mu, rwkv_w_r, rwkv_w_k, rwkv_w_v, rwkv_w_o, rwkv_decay_w0, rwkv_decay_w1, rwkv_decay_w2, rwkv_a_w0, rwkv_a_w1, rwkv_a_w2, rwkv_g_w1, rwkv_g_w2, rwkv_k_k, rwkv_k_a, rwkv_r_k, rwkv_ln_w, rwkv_ln_b, ffn_w_up, ffn_w_down):
    B, SEQ, D = x_prompt.shape
    DB, S, _ = x_sample.shape
    n_meta = meta_tokens.shape[0]
    T = n_meta + SEQ
    Tp = -(-T // SEQ_ALIGN) * SEQ_ALIGN
    page = cache_kv_latent.shape[2]
    past_len = page_table.shape[1] * page
    kv_lora = cache_kv_latent.shape[-1]
    rope = cache_k_rope.shape[-1]
    q_lora = mla_q_a_norm.shape[-1]
    H, hd = state_wkv.shape[2], state_wkv.shape[3]
    qk = mla_w_q_b.shape[-1]
    kvb = mla_w_kv_b.shape[-1]
    ov = mla_w_o.shape[1]
    heads = (qk + ov - kvb) // rope
    nope = qk // heads - rope
    vhead = ov // heads
    scale = float(nope + rope) ** -0.5
    dims = (q_lora, kv_lora, rope, heads, nope, scale)

    row = lambda vec: vec.reshape(1, -1).astype(F32)
    b16 = lambda w: w.astype(BF16)

    meta = jnp.broadcast_to(meta_tokens.astype(F32)[None], (B, n_meta, D))
    hp = jnp.concatenate([meta, x_prompt, jnp.zeros((B, Tp - T, D), F32)], axis=1)
    hs = x_sample.reshape(1, DB * S, D)

    l = 0
    wa = mla_w_qkv_a[l]
    wa = b16(jnp.concatenate([wa, _swap_halves(wa[:, q_lora + kv_lora:], rope)], axis=1))
    wqb = mla_w_q_b[l].reshape(q_lora, heads, nope + rope)
    wnope = b16(wqb[..., :nope].reshape(q_lora, heads * nope))
    wpe = wqb[..., nope:].reshape(q_lora, heads * rope)
    wpesw = b16(_swap_halves(wpe, rope))
    wpe = b16(wpe)
    wkvb = mla_w_kv_b[l].reshape(kv_lora, heads, nope + vhead)
    wuk = b16(jnp.transpose(wkvb[..., :nope], (1, 2, 0)))
    wuv = b16(jnp.transpose(wkvb[..., nope:], (1, 0, 2)))
    wo = b16(mla_w_o[l])
    mla_w = (wa, row(mla_q_a_norm[l]), row(mla_kv_a_norm[l]), wnope, wpe, wpesw, wuk)

    tabs_p = _rope_tables(jnp.arange(Tp), rope, heads)
    tabs_s = _rope_tables(jnp.tile(past_len + jnp.arange(S), DB), rope, heads)

    ql_p, qp_p, c_p, kpe_p, cb_p, kb_p = _mla_project(hp, row(norm_mix[0]), mla_w, tabs_p, dims)
    ql_s, qp_s, c_s, kpe_s, cb_s, kb_s = _mla_project(hs, row(norm_mix[0]), mla_w, tabs_s, dims)

    o_p = _attn_prompt(ql_p.reshape(B, Tp * heads, kv_lora), qp_p.reshape(B, Tp * heads, rope), cb_p, kb_p, heads)
    o_s = _attn_sample(ql_s.reshape(DB, S * heads, kv_lora), qp_s.reshape(DB, S * heads, rope),
                       cb_s.reshape(DB, S, kv_lora), kb_s.reshape(DB, S, rope),
                       cache_kv_latent, cache_k_rope, page_table, l, heads)

    hp = _mla_out(o_p.reshape(B * Tp, heads * kv_lora), hp.reshape(B * Tp, D), wuv, wo)
    hs = _mla_out(o_s.reshape(DB * S, heads * kv_lora), hs.reshape(DB * S, D), wuv, wo)

    wup0, wdn0 = b16(ffn_w_up[0]), b16(ffn_w_down[0])
    hp, np_ = _ffn(hp, row(norm_ffn[0]), wup0, wdn0, row(norm_mix[1]))
    hs, ns_ = _ffn(hs, row(norm_ffn[0]), wup0, wdn0, row(norm_mix[1]))

    np3 = np_.reshape(B, Tp, D)
    ns3 = ns_.reshape(DB, S, D)
    xprev_p = jnp.concatenate([jnp.zeros((B, 1, D), F32), np3[:, :-1]], axis=1).reshape(B * Tp, D)
    xprev_s = jnp.concatenate([state_shift[l].astype(F32)[:, None], ns3[:, :-1]], axis=1).reshape(DB * S, D)
    rw = (rwkv_mu[l].astype(F32), b16(rwkv_w_r[l]), b16(rwkv_w_k[l]), b16(rwkv_w_v[l]),
          row(rwkv_decay_w0[l]), b16(rwkv_decay_w1[l]), b16(rwkv_decay_w2[l]),
          row(rwkv_a_w0[l]), b16(rwkv_a_w1[l]), b16(rwkv_a_w2[l]),
          b16(rwkv_g_w1[l]), b16(rwkv_g_w2[l]))
    wkv_par = (row(rwkv_k_k[l]), row(rwkv_k_a[l]), row(rwkv_r_k[l]), row(rwkv_ln_w[l]), row(rwkv_ln_b[l]))

    r_p, k_p, v_p, d_p, a_p, g_p = _rwkv_project(np_, xprev_p, rw)
    r_s, k_s, v_s, d_s, a_s, g_s = _rwkv_project(ns_, xprev_s, rw)

    seq_p = lambda t: t.reshape(B, Tp, D)
    y_p, st_p = _wkv(seq_p(r_p), seq_p(k_p), seq_p(v_p), seq_p(d_p), seq_p(a_p), wkv_par,
                     jnp.zeros((B, H, hd, hd), F32), WKV_CHUNK, T)
    Ls = -(-S // 8) * 8
    seq_s = lambda t: jnp.pad(t.reshape(DB, S, D), ((0, 0), (0, Ls - S), (0, 0)))
    y_s, st_s = _wkv(seq_s(r_s), seq_s(k_s), seq_s(v_s), seq_s(d_s), seq_s(a_s), wkv_par,
                     state_wkv[l].astype(F32), Ls, S)
    y_s = y_s[:, :S].reshape(DB * S, D)

    wo_r = b16(rwkv_w_o[l])
    hp = _gated_out(y_p.reshape(B * Tp, D), g_p, hp, wo_r)
    hs = _gated_out(y_s, g_s, hs, wo_r)

    wup1, wdn1 = b16(ffn_w_up[1]), b16(ffn_w_down[1])
    _, yp = _ffn(hp, row(norm_ffn[1]), wup1, wdn1, row(norm_final))
    _, ys = _ffn(hs, row(norm_ffn[1]), wup1, wdn1, row(norm_final))

    y_prompt = yp.reshape(B, Tp, D)[:, n_meta:T]
    y_sample = ys.reshape(DB, S, D)
    return (y_prompt, y_sample,
            c_p[None, :, :T], kpe_p[None, :, :T],
            c_s.reshape(1, DB, S, kv_lora), kpe_s.reshape(1, DB, S, rope),
            st_p[None], np3[None, :, T - 1], st_s[None], ns3[None, :, S - 1])
```

```python
import functools
import math

import jax
import jax.numpy as jnp
from jax import lax
from jax.experimental import pallas as pl
from jax.experimental.pallas import tpu as pltpu

F32 = jnp.float32
BF16 = jnp.bfloat16

RMS_EPS = 1e-6
GN_EPS = 64e-5
NEG_INF = -1e30
ROPE_THETA = 10000.0
LOG2E = 1.4426950408889634

LANES = 128
VMEM_LIMIT_BYTES = 56 * 1024 * 1024

ATTN_Q_TOKENS = 256
ATTN_K_SMALL = 256
ATTN_K_BIG = 512
ATTN_SUB_ROWS = 256
SEQ_ALIGN = 256
ROW_TILE = 768
RWKV_ROW_TILE = 384
FF_TILE = 1024
WKV_CHUNK = 64
PAGES_PER_GROUP = 16


def _cparams(*sem):
    return pltpu.CompilerParams(dimension_semantics=sem, vmem_limit_bytes=VMEM_LIMIT_BYTES)


def _dot(a, b):
    return jnp.dot(a, b, preferred_element_type=F32)


def _dot_nt(a, b):
    return lax.dot_general(a, b, (((1,), (1,)), ((), ())), preferred_element_type=F32)


def _dot_tn(a, b):
    return lax.dot_general(a, b, (((0,), (0,)), ((), ())), preferred_element_type=F32)


def _rms(x, g):
    return x * lax.rsqrt(jnp.mean(x * x, axis=-1, keepdims=True) + RMS_EPS) * g


def _row_tile(n, target=ROW_TILE, align=8):
    best = None
    for t in range(align, min(n, target) + 1, align):
        if n % t == 0:
            best = t
    assert best is not None, n
    return best


def _mla_proj_kernel(x_ref, g_ref, wa_ref, wkvT_ref, wkT_ref, qn_ref, kvn_ref, kvnc_ref, wnope_ref, wpe_ref,
                     wpesw_ref, wuk_ref, cosq_ref, sinq_ref, cosk_ref, sink_ref,
                     q_ref, ckv_ref, kpeT_ref, ckvb_ref, kT_ref,
                     *, q_lora, kv_lora, rope, heads, nope, scale):
    tm = x_ref.shape[1]
    qk_w = kv_lora + LANES
    n = _rms(x_ref[0], g_ref[...]).astype(BF16)
    a = _dot(n, wa_ref[...])
    c_q = _rms(a[:, :q_lora], qn_ref[...]).astype(BF16)
    c_kv = _rms(a[:, q_lora:], kvn_ref[...])
    ckv_ref[0] = c_kv
    ckvb_ref[0] = c_kv.astype(BF16)
    a_kv_t = _dot_nt(wkvT_ref[...], n)
    c_kv_t = a_kv_t * lax.rsqrt(jnp.mean(a_kv_t * a_kv_t, axis=0, keepdims=True) + RMS_EPS) * kvnc_ref[...]
    a_t = _dot_nt(wkT_ref[...], n)
    k_pe_t = a_t[:rope] * cosk_ref[...] + a_t[rope:] * sink_ref[...]
    kpeT_ref[0] = k_pe_t
    kT_ref[0] = jnp.concatenate([c_kv_t.astype(BF16), k_pe_t.astype(BF16),
                                 jnp.zeros((LANES - rope, tm), BF16)], axis=0)
    cosq = jnp.tile(cosq_ref[...], (1, heads))
    sinq = jnp.tile(sinq_ref[...], (1, heads))
    q_pe = _dot(c_q, wpe_ref[...]) * cosq + _dot(c_q, wpesw_ref[...]) * sinq
    q_pe = (q_pe * scale).astype(BF16)
    q_nope = _dot(c_q, wnope_ref[...]).astype(BF16)
    for h in range(heads):
        q_lat = _dot(q_nope[:, h * nope:(h + 1) * nope], wuk_ref[h])
        q_ref[0, :, h * qk_w:h * qk_w + kv_lora] = (q_lat * scale).astype(BF16)
        q_ref[0, :, h * qk_w + kv_lora:(h + 1) * qk_w] = q_pe[:, h * LANES:(h + 1) * LANES]


def _mla_project(x, g, wts, tabs, dims):
    B, T, D = x.shape
    q_lora, kv_lora, rope, heads, nope, scale = dims
    qk_w = kv_lora + LANES
    tm = _row_tile(T, align=LANES)
    wa, wkvT, wkT, qn, kvn, kvnc, wnope, wpe, wpesw, wuk = wts
    cosq, sinq, cosk, sink = tabs
    full = lambda arr: pl.BlockSpec(arr.shape, lambda b, i: (0,) * arr.ndim)
    row = lambda w: pl.BlockSpec((1, tm, w), lambda b, i: (b, i, 0))
    col = lambda r: pl.BlockSpec((1, r, tm), lambda b, i: (b, 0, i))
    kern = functools.partial(_mla_proj_kernel, q_lora=q_lora, kv_lora=kv_lora, rope=rope,
                             heads=heads, nope=nope, scale=scale)
    return pl.pallas_call(
        kern,
        grid=(B, T // tm),
        in_specs=[row(D), full(g), full(wa), full(wkvT), full(wkT), full(qn), full(kvn), full(kvnc), full(wnope),
                  full(wpe), full(wpesw), full(wuk),
                  pl.BlockSpec((tm, LANES), lambda b, i: (i, 0)),
                  pl.BlockSpec((tm, LANES), lambda b, i: (i, 0)),
                  pl.BlockSpec((rope, tm), lambda b, i: (0, i)),
                  pl.BlockSpec((rope, tm), lambda b, i: (0, i))],
        out_specs=[row(heads * qk_w), row(kv_lora), col(rope), row(kv_lora), col(qk_w)],
        out_shape=[jax.ShapeDtypeStruct((B, T, heads * qk_w), BF16),
                   jax.ShapeDtypeStruct((B, T, kv_lora), F32),
                   jax.ShapeDtypeStruct((B, rope, T), F32),
                   jax.ShapeDtypeStruct((B, T, kv_lora), BF16),
                   jax.ShapeDtypeStruct((B, qk_w, T), BF16)],
        compiler_params=_cparams("parallel", "parallel"),
        name="mla_project",
    )(x, g, wa, wkvT, wkT, qn, kvn, kvnc, wnope, wpe, wpesw, wuk, cosq, sinq, cosk, sink)


def _softmax_update(s, vals, m_scr, l_scr, acc_scr, rows=slice(None)):
    m_prev = m_scr[rows, :]
    m_next = jnp.maximum(m_prev, jnp.max(s, axis=1, keepdims=True))
    p = jnp.exp2(s - jnp.tile(m_next, (1, s.shape[1] // LANES)))
    alpha = jnp.exp2(m_prev - m_next)
    l_scr[rows, :] = alpha * l_scr[rows, :] + jnp.sum(p, axis=1, keepdims=True)
    acc_scr[rows, :] = (acc_scr[rows, :] * jnp.tile(alpha, (1, acc_scr.shape[1] // LANES))
                        + _dot(p.astype(BF16), vals))
    m_scr[rows, :] = m_next


def _attn_prompt_kernel(q_ref, c_ref, kT_ref, o_ref, q_scr, m_scr, l_scr, acc_scr,
                        *, heads, tq, small, big, kv_lora, sub_rows):
    qi = pl.program_id(1)
    rows = tq * heads
    qk_w = kv_lora + LANES
    for h in range(heads):
        q_scr[h * tq:(h + 1) * tq, :] = q_ref[0, :, h * qk_w:(h + 1) * qk_w]
    m_scr[...] = jnp.full(m_scr.shape, -jnp.inf, F32)
    l_scr[...] = jnp.zeros(l_scr.shape, F32)
    acc_scr[...] = jnp.zeros(acc_scr.shape, F32)
    per_big = big // small
    n_sub = rows // sub_rows

    def chunk(j, n_small, masked):
        tk = n_small * small
        start = pl.multiple_of(j * small, small)
        kc = c_ref[0, pl.ds(start, tk), :]
        kT = jnp.concatenate([kT_ref[0, j + i] for i in range(n_small)], axis=1)
        score = lambda i: _dot(q_scr[i * sub_rows:(i + 1) * sub_rows, :], kT)
        s_next = score(0)
        for i in range(n_sub):
            s = s_next
            if i + 1 < n_sub:
                s_next = score(i + 1)
            if masked:
                row = i * sub_rows + lax.broadcasted_iota(jnp.int32, (sub_rows, tk), 0)
                key = start + lax.broadcasted_iota(jnp.int32, (sub_rows, tk), 1)
                s = jnp.where(key <= qi * tq + (row & (tq - 1)), s, NEG_INF)
            _softmax_update(s, kc, m_scr, l_scr, acc_scr, slice(i * sub_rows, (i + 1) * sub_rows))

    n_full = (qi * tq + 1) // small
    n_total = (qi * tq + tq - 1) // small + 1
    n_big = n_full // per_big

    def big_body(j, carry):
        chunk(j * per_big, per_big, False)
        return carry

    def small_body(j, carry):
        chunk(j, 1, False)
        return carry

    def masked_body(j, carry):
        chunk(j, 1, True)
        return carry

    lax.fori_loop(0, n_big, big_body, 0)
    lax.fori_loop(n_big * per_big, n_full, small_body, 0)
    lax.fori_loop(n_full, n_total, masked_body, 0)
    inv = 1.0 / l_scr[...]
    o = acc_scr[...] * jnp.tile(inv, (1, kv_lora // LANES))
    for h in range(heads):
        o_ref[0, :, h * kv_lora:(h + 1) * kv_lora] = o[h * tq:(h + 1) * tq].astype(o_ref.dtype)


def _attn_prompt(q, c_kv, k_t, heads):
    B, T, C = c_kv.shape
    qk_w = C + LANES
    tq, small, big = ATTN_Q_TOKENS, ATTN_K_SMALL, ATTN_K_BIG
    assert T % small == 0 and big % small == 0 and small % tq == 0 and tq & (tq - 1) == 0 and C % LANES == 0
    rows = tq * heads
    kern = functools.partial(_attn_prompt_kernel, heads=heads, tq=tq, small=small, big=big, kv_lora=C,
                             sub_rows=math.gcd(rows, ATTN_SUB_ROWS))
    return pl.pallas_call(
        kern,
        grid=(B, T // tq),
        in_specs=[pl.BlockSpec((1, tq, heads * qk_w), lambda b, i: (b, i, 0)),
                  pl.BlockSpec((1, T, C), lambda b, i: (b, 0, 0)),
                  pl.BlockSpec((1, T // small, qk_w, small), lambda b, i: (b, 0, 0, 0))],
        out_specs=pl.BlockSpec((1, tq, heads * C), lambda b, i: (b, i, 0)),
        out_shape=jax.ShapeDtypeStruct((B, T, heads * C), BF16),
        scratch_shapes=[pltpu.VMEM((rows, qk_w), BF16),
                        pltpu.VMEM((rows, LANES), F32), pltpu.VMEM((rows, LANES), F32),
                        pltpu.VMEM((rows, C), F32)],
        compiler_params=_cparams("parallel", "arbitrary"),
        name="attn_prompt",
    )(q, c_kv, k_t)


def _attn_sample_kernel(pt_ref, ql_ref, qp_ref, cn_ref, knT_ref, cache_c, cache_kT, o_ref,
                        cbuf, kbuf, sem, m_scr, l_scr, acc_scr,
                        *, layer, group, n_groups, page, heads):
    b = pl.program_id(0)
    total = pl.num_programs(0) * n_groups

    def copies(gid):
        slot = gid % 2
        out = []
        for g in range(group):
            pg = pt_ref[gid * group + g]
            out.append(pltpu.make_async_copy(cache_c.at[layer, pg], cbuf.at[slot, pl.ds(g * page, page), :],
                                             sem.at[0, slot]))
            out.append(pltpu.make_async_copy(cache_kT.at[layer, pg], kbuf.at[slot, :, pl.ds(g * page, page)],
                                             sem.at[1, slot]))
        return out

    @pl.when(b == 0)
    def _():
        for cp in copies(0):
            cp.start()

    m_scr[...] = jnp.full(m_scr.shape, -jnp.inf, F32)
    l_scr[...] = jnp.zeros(l_scr.shape, F32)
    acc_scr[...] = jnp.zeros(acc_scr.shape, F32)
    ql = ql_ref[0]
    qp = qp_ref[0]

    def body(j, carry):
        gid = b * n_groups + j

        @pl.when(gid + 1 < total)
        def _():
            for cp in copies(gid + 1):
                cp.start()

        for cp in copies(gid):
            cp.wait()
        slot = gid % 2
        kc = cbuf[slot].astype(BF16)
        kT = kbuf[slot].astype(BF16)
        _softmax_update(_dot_nt(ql, kc) + _dot(qp, kT), kc, m_scr, l_scr, acc_scr)
        return carry

    lax.fori_loop(0, n_groups, body, 0)

    cn = cn_ref[0]
    s = _dot_nt(ql, cn) + _dot(qp, knT_ref[0])
    rows = s.shape[0]
    tok = lax.broadcasted_iota(jnp.int32, (rows, LANES), 0) // heads
    key = lax.broadcasted_iota(jnp.int32, (rows, LANES), 1)
    _softmax_update(jnp.where(key <= tok, s, NEG_INF), cn, m_scr, l_scr, acc_scr)
    inv = 1.0 / l_scr[...]
    o_ref[0] = (acc_scr[...] * jnp.tile(inv, (1, acc_scr.shape[1] // LANES))).astype(o_ref.dtype)


def _attn_sample(q_lat, q_pe, c_new, k_new_t, cache_c, cache_kT, page_table, layer, heads):
    DB, rows, C = q_lat.shape
    R = cache_kT.shape[2]
    page = cache_c.shape[2]
    n_pages = page_table.shape[1]
    assert page == LANES and rows // heads <= LANES
    group = math.gcd(PAGES_PER_GROUP, n_pages)
    n_groups = n_pages // group
    pt = page_table.reshape(-1).astype(jnp.int32)
    per_b = lambda shape: pl.BlockSpec((1,) + shape, lambda b, pt_ref: (b, 0, 0))
    kern = functools.partial(_attn_sample_kernel, layer=layer, group=group, n_groups=n_groups, page=page,
                             heads=heads)
    grid_spec = pltpu.PrefetchScalarGridSpec(
        num_scalar_prefetch=1,
        grid=(DB,),
        in_specs=[per_b((rows, C)), per_b((rows, R)), per_b((LANES, C)), per_b((R, LANES)),
                  pl.BlockSpec(memory_space=pl.ANY), pl.BlockSpec(memory_space=pl.ANY)],
        out_specs=per_b((rows, C)),
        scratch_shapes=[pltpu.VMEM((2, group * page, C), F32), pltpu.VMEM((2, R, group * page), F32),
                        pltpu.SemaphoreType.DMA((2, 2)),
                        pltpu.VMEM((rows, LANES), F32), pltpu.VMEM((rows, LANES), F32),
                        pltpu.VMEM((rows, C), F32)],
    )
    return pl.pallas_call(
        kern,
        grid_spec=grid_spec,
        out_shape=jax.ShapeDtypeStruct((DB, rows, C), BF16),
        compiler_params=_cparams("arbitrary"),
        name="attn_sample",
    )(pt, q_lat, q_pe, c_new, k_new_t, cache_c, cache_kT)


def _mla_out_kernel(o_ref, x_ref, wuv_ref, wo_ref, h_ref, cat_scr, *, heads, kv_lora, vhead):
    for h in range(heads):
        oh = _dot(o_ref[:, h * kv_lora:(h + 1) * kv_lora], wuv_ref[h])
        cat_scr[:, h * vhead:(h + 1) * vhead] = oh.astype(BF16)
    h_ref[...] = x_ref[...] + _dot(cat_scr[...], wo_ref[...])


def _mla_out(o_lat, x, wuv, wo):
    N, D = x.shape
    heads, kv_lora, vhead = wuv.shape
    tm = _row_tile(N)
    kern = functools.partial(_mla_out_kernel, heads=heads, kv_lora=kv_lora, vhead=vhead)
    return pl.pallas_call(
        kern,
        grid=(N // tm,),
        in_specs=[pl.BlockSpec((tm, heads * kv_lora), lambda i: (i, 0)),
                  pl.BlockSpec((tm, D), lambda i: (i, 0)),
                  pl.BlockSpec(wuv.shape, lambda i: (0, 0, 0)),
                  pl.BlockSpec(wo.shape, lambda i: (0, 0))],
        out_specs=pl.BlockSpec((tm, D), lambda i: (i, 0)),
        out_shape=jax.ShapeDtypeStruct((N, D), F32),
        scratch_shapes=[pltpu.VMEM((tm, heads * vhead), BF16)],
        compiler_params=_cparams("parallel"),
        name="mla_out",
    )(o_lat, x, wuv, wo)


def _ffn_kernel(h_ref, g_ref, wup_ref, wdn_ref, gnext_ref, hout_ref, nnext_ref, xn_scr, acc_scr):
    k = pl.program_id(1)

    @pl.when(k == 0)
    def _():
        xn_scr[...] = _rms(h_ref[...], g_ref[...]).astype(BF16)
        acc_scr[...] = jnp.zeros(acc_scr.shape, F32)

    u = jnp.maximum(_dot(xn_scr[...], wup_ref[...]), 0.0)
    acc_scr[...] += _dot((u * u).astype(BF16), wdn_ref[...])

    @pl.when(k == pl.num_programs(1) - 1)
    def _():
        out = h_ref[...] + acc_scr[...]
        hout_ref[...] = out
        nnext_ref[...] = _rms(out, gnext_ref[...])


def _ffn(h, g, wup, wdn, g_next):
    N, D = h.shape
    FF = wup.shape[1]
    tm = _row_tile(N)
    tf = min(FF_TILE, FF)
    return pl.pallas_call(
        _ffn_kernel,
        grid=(N // tm, FF // tf),
        in_specs=[pl.BlockSpec((tm, D), lambda i, k: (i, 0)),
                  pl.BlockSpec((1, D), lambda i, k: (0, 0)),
                  pl.BlockSpec((D, tf), lambda i, k: (0, k)),
                  pl.BlockSpec((tf, D), lambda i, k: (k, 0)),
                  pl.BlockSpec((1, D), lambda i, k: (0, 0))],
        out_specs=[pl.BlockSpec((tm, D), lambda i, k: (i, 0)),
                   pl.BlockSpec((tm, D), lambda i, k: (i, 0))],
        out_shape=[jax.ShapeDtypeStruct((N, D), F32), jax.ShapeDtypeStruct((N, D), F32)],
        scratch_shapes=[pltpu.VMEM((tm, D), BF16), pltpu.VMEM((tm, D), F32)],
        compiler_params=_cparams("parallel", "arbitrary"),
        name="ffn",
    )(h, g, wup, wdn, g_next)


def _rwkv_proj_kernel(n_ref, xp_ref, mu_ref, wr_ref, wk_ref, wv_ref, dw0_ref, dw1_ref, dw2_ref,
                      aw0_ref, aw1_ref, aw2_ref, gw1_ref, gw2_ref,
                      r_ref, k_ref, v_ref, d_ref, a_ref, g_ref):
    n = n_ref[...]
    xx = xp_ref[...] - n
    mix = lambda j: (n + xx * mu_ref[j:j + 1, :]).astype(BF16)
    r_ref[...] = _dot(mix(0), wr_ref[...])
    k_ref[...] = _dot(mix(2), wk_ref[...])
    v_ref[...] = _dot(mix(3), wv_ref[...])
    z = dw0_ref[...] + _dot(jnp.tanh(_dot(mix(1), dw1_ref[...])).astype(BF16), dw2_ref[...])
    d_ref[...] = (-math.exp(-0.5)) / (1.0 + jnp.exp(-z))
    za = aw0_ref[...] + _dot(_dot(mix(4), aw1_ref[...]).astype(BF16), aw2_ref[...])
    a_ref[...] = 1.0 / (1.0 + jnp.exp(-za))
    zg = _dot(mix(5), gw1_ref[...])
    g_ref[...] = _dot((1.0 / (1.0 + jnp.exp(-zg))).astype(BF16), gw2_ref[...])


def _rwkv_project(n, x_prev, wts):
    N, D = n.shape
    tm = _row_tile(N, RWKV_ROW_TILE)
    row = pl.BlockSpec((tm, D), lambda i: (i, 0))
    full = lambda arr: pl.BlockSpec(arr.shape, lambda i: (0,) * arr.ndim)
    return pl.pallas_call(
        _rwkv_proj_kernel,
        grid=(N // tm,),
        in_specs=[row, row] + [full(w) for w in wts],
        out_specs=[row] * 6,
        out_shape=[jax.ShapeDtypeStruct((N, D), F32)] * 6,
        compiler_params=_cparams("parallel"),
        name="rwkv_project",
    )(n, x_prev, *wts)


def _wkv_kernel(r_ref, k_ref, v_ref, d_ref, a_ref, kk_ref, ka_ref, rk_ref, lnw_ref, lnb_ref, s0_ref,
                y_ref, sout_ref, st_scr, *, L, t_valid, pairs, hd):
    c = pl.program_id(1)
    L2 = 2 * L
    W = 2 * hd
    P = range(pairs)
    m0 = lax.broadcasted_iota(jnp.int32, (1, W), 1) < hd

    @pl.when(c == 0)
    def _():
        z = jnp.zeros((hd, hd), F32)
        for p in P:
            s_bd = jnp.concatenate([jnp.concatenate([s0_ref[0, 2 * p], z], axis=1),
                                    jnp.concatenate([z, s0_ref[0, 2 * p + 1]], axis=1)], axis=0)
            st_scr[p] = s_bd.T

    valid = (c * L + lax.broadcasted_iota(jnp.int32, (L, 1), 0)) < t_valid
    tri = (lax.broadcasted_iota(jnp.int32, (L, L), 0) >= lax.broadcasted_iota(jnp.int32, (L, L), 1))
    tri3 = jnp.tile(tri.astype(BF16), (1, 3))
    ones3 = jnp.ones((3 * L, W), BF16)
    i2 = lax.broadcasted_iota(jnp.int32, (L2, L2), 0)
    j2 = lax.broadcasted_iota(jnp.int32, (L2, L2), 1)
    mask_s = j2 < i2
    mask_i = j2 <= i2
    eye = (i2 == j2).astype(F32)

    def stack(x):
        return jnp.concatenate([jnp.where(m0, x, 0.0), jnp.where(m0, 0.0, x)], axis=0).astype(BF16)

    def head_sum(x):
        s_a = jnp.sum(jnp.where(m0, x, 0.0), axis=1, keepdims=True)
        s_b = jnp.sum(jnp.where(m0, 0.0, x), axis=1, keepdims=True)
        return jnp.where(m0, s_a, s_b)

    def split3(x):
        hi = x.astype(BF16)
        r1 = x - hi.astype(F32)
        mid = r1.astype(BF16)
        lo = (r1 - mid.astype(F32)).astype(BF16)
        return jnp.concatenate([hi, mid, lo], axis=0)

    sl = [slice(p * W, (p + 1) * W) for p in P]
    load = lambda ref, p: jnp.where(valid, ref[0, :, sl[p]], 0.0)
    r = [load(r_ref, p) for p in P]
    k = [load(k_ref, p) for p in P]
    v = [load(v_ref, p) for p in P]
    d = [load(d_ref, p) for p in P]
    a = [load(a_ref, p) for p in P]
    d3 = [split3(d[p]) for p in P]
    cum = [_dot(tri3, d3[p]) for p in P]
    kk = [k[p] * kk_ref[:, sl[p]] for p in P]
    kk = [kk[p] / jnp.maximum(jnp.sqrt(head_sum(kk[p] * kk[p])), 1e-12) for p in P]
    kp = [k[p] * (1.0 + (a[p] - 1.0) * ka_ref[:, sl[p]]) for p in P]
    bv = [kk[p] * a[p] for p in P]
    e_neg = [jnp.exp(-cum[p]) for p in P]
    a_st = [stack(-kk[p] * jnp.exp(cum[p] - d[p])) for p in P]
    r_st = [stack(r[p] * jnp.exp(cum[p])) for p in P]
    b_st = [stack(bv[p] * e_neg[p]) for p in P]
    k_st = [stack(kp[p] * e_neg[p]) for p in P]
    v_st = [stack(v[p]) for p in P]
    pm = [_dot_nt(jnp.concatenate([a_st[p], r_st[p]], axis=0), jnp.concatenate([b_st[p], k_st[p]], axis=0))
          for p in P]
    m_ab = [jnp.where(mask_s, pm[p][:L2, :L2], 0.0) for p in P]
    m_ak = [jnp.where(mask_s, pm[p][:L2, L2:], 0.0).astype(BF16) for p in P]
    m_rb = [jnp.where(mask_i, pm[p][L2:, :L2], 0.0).astype(BF16) for p in P]
    m_rk = [jnp.where(mask_i, pm[p][L2:, L2:], 0.0).astype(BF16) for p in P]
    st = [st_scr[p] for p in P]
    stb = [st[p].astype(BF16) for p in P]
    rhs = [_dot(jnp.concatenate([a_st[p], m_ak[p]], axis=1), jnp.concatenate([stb[p], v_st[p]], axis=0)) for p in P]
    inv = [eye + m_ab[p] for p in P]
    pw = [m_ab[p].astype(BF16) for p in P]
    pw = [_dot(pw[p], pw[p]).astype(BF16) for p in P]
    levels = int(math.log2(L))
    for lvl in range(1, levels):
        last = lvl == levels - 1
        z = [_dot(pw[p], inv[p].astype(BF16) if last else jnp.concatenate([inv[p].astype(BF16), pw[p]], axis=1))
             for p in P]
        inv = [inv[p] + z[p][:, :L2] for p in P]
        if not last:
            pw = [z[p][:, L2:].astype(BF16) for p in P]
    u_st = [_dot(inv[p].astype(BF16), rhs[p].astype(BF16)).astype(BF16) for p in P]
    y_st = [_dot(jnp.concatenate([r_st[p], m_rb[p], m_rk[p]], axis=1),
                 jnp.concatenate([stb[p], u_st[p], v_st[p]], axis=0)) for p in P]
    decay_col = [jnp.exp(_dot_tn(d3[p], ones3)) for p in P]
    e_rest = [jnp.exp(cum[p][L - 1:L, :] - cum[p]) for p in P]
    for p in P:
        st_scr[p] = decay_col[p] * st[p] + _dot_tn(
            jnp.concatenate([stack(bv[p] * e_rest[p]), stack(kp[p] * e_rest[p])], axis=0),
            jnp.concatenate([u_st[p], v_st[p]], axis=0))
    for p in P:
        y = y_st[p][:L] + y_st[p][L:]
        mean = head_sum(y) * (1.0 / hd)
        yc = y - mean
        var = head_sum(yc * yc) * (1.0 / hd)
        yn = yc * lax.rsqrt(var + GN_EPS) * lnw_ref[:, sl[p]] + lnb_ref[:, sl[p]]
        y_ref[0, :, sl[p]] = yn + head_sum(r[p] * kp[p] * rk_ref[:, sl[p]]) * v[p]

    @pl.when(c == pl.num_programs(1) - 1)
    def _():
        for p in P:
            s_bd = st_scr[p].T
            sout_ref[0, 2 * p] = s_bd[:hd, :hd]
            sout_ref[0, 2 * p + 1] = s_bd[hd:, hd:]


def _wkv(r, k, v, d, a, params, s0, L, t_valid):
    B, T, D = r.shape
    H, hd = s0.shape[1], s0.shape[2]
    assert 2 * hd == LANES and H % 2 == 0 and T % L == 0 and (2 * L) % LANES == 0
    pairs = H // 2
    seq = pl.BlockSpec((1, L, D), lambda b, c: (b, c, 0))
    vec = pl.BlockSpec((1, D), lambda b, c: (0, 0))
    state = pl.BlockSpec((1, H, hd, hd), lambda b, c: (b, 0, 0, 0))
    kern = functools.partial(_wkv_kernel, L=L, t_valid=t_valid, pairs=pairs, hd=hd)
    return pl.pallas_call(
        kern,
        grid=(B, T // L),
        in_specs=[seq] * 5 + [vec] * 5 + [state],
        out_specs=[seq, state],
        out_shape=[jax.ShapeDtypeStruct((B, T, D), F32), jax.ShapeDtypeStruct(s0.shape, F32)],
        scratch_shapes=[pltpu.VMEM((pairs, LANES, LANES), F32)],
        compiler_params=_cparams("parallel", "arbitrary"),
        name="wkv",
    )(r, k, v, d, a, *params, s0)


def _wkv_sample_kernel(r_ref, k_ref, v_ref, d_ref, a_ref, kk_ref, ka_ref, rk_ref, lnw_ref, lnb_ref, s0_ref,
                       y_ref, sout_ref, w_scr, a_scr, b_scr, kp_scr, yraw_scr, *, steps, hd):
    for t in range(steps):
        kt = k_ref[t]
        at = a_ref[t]
        kk = kt * kk_ref[...]
        kk = kk / jnp.maximum(jnp.sqrt(jnp.sum(kk * kk, axis=0, keepdims=True)), 1e-12)
        w_scr[t] = jnp.exp(d_ref[t])
        a_scr[t] = -kk
        b_scr[t] = kk * at
        kp_scr[t] = kt * (1.0 + (at - 1.0) * ka_ref[...])

    def body(vi, carry):
        s = s0_ref[vi]
        for t in range(steps):
            sa = jnp.sum(s * a_scr[t], axis=0, keepdims=True)
            s = s * w_scr[t] + sa * b_scr[t] + v_ref[t, pl.ds(vi, 1), :] * kp_scr[t]
            yraw_scr[t, pl.ds(vi, 1), :] = jnp.sum(s * r_ref[t], axis=0, keepdims=True)
        sout_ref[vi] = s
        return carry

    lax.fori_loop(0, hd, body, 0)
    for t in range(steps):
        y = yraw_scr[t]
        mean = jnp.mean(y, axis=0, keepdims=True)
        yc = y - mean
        var = jnp.mean(yc * yc, axis=0, keepdims=True)
        yn = yc * lax.rsqrt(var + GN_EPS) * lnw_ref[...] + lnb_ref[...]
        bonus = jnp.sum(r_ref[t] * kp_scr[t] * rk_ref[...], axis=0, keepdims=True)
        y_ref[t] = yn + bonus * v_ref[t]


def _wkv_sample(r, k, v, d, a, params, s0):
    S, D, DB = r.shape
    H, hd = s0.shape[0], s0.shape[1]
    seq = pl.BlockSpec((S, hd, DB), lambda h: (0, h, 0))
    vec = pl.BlockSpec((hd, DB), lambda h: (h, 0))
    state = pl.BlockSpec((None, hd, hd, DB), lambda h: (h, 0, 0, 0))
    kern = functools.partial(_wkv_sample_kernel, steps=S, hd=hd)
    return pl.pallas_call(
        kern,
        grid=(H,),
        in_specs=[seq] * 5 + [vec] * 5 + [state],
        out_specs=[seq, state],
        out_shape=[jax.ShapeDtypeStruct((S, D, DB), F32), jax.ShapeDtypeStruct(s0.shape, F32)],
        scratch_shapes=[pltpu.VMEM((S, hd, DB), F32)] * 5,
        compiler_params=_cparams("parallel"),
        name="wkv_sample",
    )(r, k, v, d, a, *params, s0)


def _gated_out_kernel(y_ref, g_ref, h_ref, wo_ref, o_ref):
    o_ref[...] = h_ref[...] + _dot((y_ref[...] * g_ref[...]).astype(BF16), wo_ref[...])


def _gated_out(y, g, h, wo):
    N, D = h.shape
    tm = _row_tile(N)
    row = pl.BlockSpec((tm, D), lambda i: (i, 0))
    return pl.pallas_call(
        _gated_out_kernel,
        grid=(N // tm,),
        in_specs=[row, row, row, pl.BlockSpec(wo.shape, lambda i: (0, 0))],
        out_specs=row,
        out_shape=jax.ShapeDtypeStruct((N, D), F32),
        compiler_params=_cparams("parallel"),
        name="rwkv_out",
    )(y, g, h, wo)


def _rope_tables(pos, rope, heads):
    half = rope // 2
    inv_freq = ROPE_THETA ** (-jnp.arange(half, dtype=F32) / half)
    ang = pos.astype(F32)[:, None] * inv_freq[None, :]
    cos, sin = jnp.cos(ang), jnp.sin(ang)
    cos2 = jnp.concatenate([cos, cos], axis=-1)
    sin2 = jnp.concatenate([-sin, sin], axis=-1)
    pad = lambda t: jnp.pad(t, ((0, 0), (0, LANES - rope)))
    return pad(cos2), pad(sin2), cos2.T, sin2.T


def _swap_halves(w, width):
    lead = w.shape[:-1]
    g = w.reshape(lead + (-1, 2, width // 2))
    return g[..., ::-1, :].reshape(w.shape)


def _pad_heads(w, heads, rope):
    K = w.shape[0]
    return jnp.pad(w.reshape(K, heads, rope), ((0, 0), (0, 0), (0, LANES - rope))).reshape(K, heads * LANES)


def kernel(x_prompt, x_sample, cache_kv_latent, cache_k_rope, state_wkv, state_shift, page_table, meta_tokens, norm_mix, norm_ffn, norm_final, mla_w_qkv_a, mla_q_a_norm, mla_kv_a_norm, mla_w_q_b, mla_w_kv_b, mla_w_o, rwkv_mu, rwkv_w_r, rwkv_w_k, rwkv_w_v, rwkv_w_o, rwkv_decay_w0, rwkv_decay_w1, rwkv_decay_w2, rwkv_a_w0, rwkv_a_w1, rwkv_a_w2, rwkv_g_w1, rwkv_g_w2, rwkv_k_k, rwkv_k_a, rwkv_r_k, rwkv_ln_w, rwkv_ln_b, ffn_w_up, ffn_w_down):
    B, SEQ, D = x_prompt.shape
    DB, S, _ = x_sample.shape
    n_meta = meta_tokens.shape[0]
    T = n_meta + SEQ
    Tp = -(-T // SEQ_ALIGN) * SEQ_ALIGN
    page = cache_kv_latent.shape[2]
    past_len = page_table.shape[1] * page
    kv_lora = cache_kv_latent.shape[-1]
    rope = cache_k_rope.shape[-1]
    q_lora = mla_q_a_norm.shape[-1]
    H, hd = state_wkv.shape[2], state_wkv.shape[3]
    qk = mla_w_q_b.shape[-1]
    kvb = mla_w_kv_b.shape[-1]
    ov = mla_w_o.shape[1]
    heads = (qk + ov - kvb) // rope
    nope = qk // heads - rope
    vhead = ov // heads
    qk_w = kv_lora + LANES
    scale = float(nope + rope) ** -0.5 * LOG2E
    dims = (q_lora, kv_lora, rope, heads, nope, scale)

    row = lambda vec: vec.reshape(1, -1).astype(F32)
    b16 = lambda w: w.astype(BF16)

    meta = jnp.broadcast_to(meta_tokens.astype(F32)[None], (B, n_meta, D))
    hp = jnp.concatenate([meta, x_prompt, jnp.zeros((B, Tp - T, D), F32)], axis=1)
    hs = x_sample.reshape(1, DB * S, D)

    l = 0
    wqkv = mla_w_qkv_a[l]
    wa = b16(wqkv[:, :q_lora + kv_lora])
    wkvT = b16(wqkv[:, q_lora:q_lora + kv_lora].T)
    wk = wqkv[:, q_lora + kv_lora:]
    wkT = b16(jnp.concatenate([wk, _swap_halves(wk, rope)], axis=1).T)
    wqb = mla_w_q_b[l].reshape(q_lora, heads, nope + rope)
    wnope = b16(wqb[..., :nope].reshape(q_lora, heads * nope))
    wpe = wqb[..., nope:].reshape(q_lora, heads * rope)
    wpesw = b16(_pad_heads(_swap_halves(wpe, rope), heads, rope))
    wpe = b16(_pad_heads(wpe, heads, rope))
    wkvb = mla_w_kv_b[l].reshape(kv_lora, heads, nope + vhead)
    wuk = b16(jnp.transpose(wkvb[..., :nope], (1, 2, 0)))
    wuv = b16(jnp.transpose(wkvb[..., nope:], (1, 0, 2)))
    wo = b16(mla_w_o[l])
    kvn = mla_kv_a_norm[l].astype(F32)
    mla_w = (wa, wkvT, wkT, row(mla_q_a_norm[l]), row(kvn), kvn.reshape(-1, 1), wnope, wpe, wpesw, wuk)

    tabs_p = _rope_tables(jnp.arange(Tp), rope, heads)
    tabs_s = _rope_tables(jnp.tile(past_len + jnp.arange(S), DB), rope, heads)

    q_p, c_p, kpeT_p, cb_p, kT_p = _mla_project(hp, row(norm_mix[0]), mla_w, tabs_p, dims)
    q_s, c_s, kpeT_s, cb_s, kT_s = _mla_project(hs, row(norm_mix[0]), mla_w, tabs_s, dims)

    ks = ATTN_K_SMALL
    kT_chunks = kT_p.reshape(B, qk_w, Tp // ks, ks).transpose(0, 2, 1, 3)
    o_p = _attn_prompt(q_p, cb_p, kT_chunks, heads)

    q_s = q_s.reshape(DB, S * heads, qk_w)
    c_new = jnp.pad(cb_s.reshape(DB, S, kv_lora), ((0, 0), (0, LANES - S), (0, 0)))
    k_new_t = kT_s[0, kv_lora:kv_lora + rope].reshape(rope, DB, S).transpose(1, 0, 2)
    k_new_t = jnp.pad(k_new_t, ((0, 0), (0, 0), (0, LANES - S)))
    o_s = _attn_sample(q_s[..., :kv_lora], q_s[..., kv_lora:kv_lora + rope], c_new, k_new_t,
                       cache_kv_latent, jnp.swapaxes(cache_k_rope, 2, 3), page_table, l, heads)

    hp = _mla_out(o_p.reshape(B * Tp, heads * kv_lora), hp.reshape(B * Tp, D), wuv, wo)
    hs = _mla_out(o_s.reshape(DB * S, heads * kv_lora), hs.reshape(DB * S, D), wuv, wo)

    wup0, wdn0 = b16(ffn_w_up[0]), b16(ffn_w_down[0])
    hp, np_ = _ffn(hp, row(norm_ffn[0]), wup0, wdn0, row(norm_mix[1]))
    hs, ns_ = _ffn(hs, row(norm_ffn[0]), wup0, wdn0, row(norm_mix[1]))

    np3 = np_.reshape(B, Tp, D)
    ns3 = ns_.reshape(DB, S, D)
    xprev_p = jnp.concatenate([jnp.zeros((B, 1, D), F32), np3[:, :-1]], axis=1).reshape(B * Tp, D)
    xprev_s = jnp.concatenate([state_shift[l].astype(F32)[:, None], ns3[:, :-1]], axis=1).reshape(DB * S, D)
    rw = (rwkv_mu[l].astype(F32), b16(rwkv_w_r[l]), b16(rwkv_w_k[l]), b16(rwkv_w_v[l]),
          row(rwkv_decay_w0[l]), b16(rwkv_decay_w1[l]), b16(rwkv_decay_w2[l]),
          row(rwkv_a_w0[l]), b16(rwkv_a_w1[l]), b16(rwkv_a_w2[l]),
          b16(rwkv_g_w1[l]), b16(rwkv_g_w2[l]))
    par = (rwkv_k_k[l], rwkv_k_a[l], rwkv_r_k[l], rwkv_ln_w[l], rwkv_ln_b[l])

    r_p, k_p, v_p, d_p, a_p, g_p = _rwkv_project(np_, xprev_p, rw)
    r_s, k_s, v_s, d_s, a_s, g_s = _rwkv_project(ns_, xprev_s, rw)

    seq_p = lambda t: t.reshape(B, Tp, D)
    y_p, st_p = _wkv(seq_p(r_p), seq_p(k_p), seq_p(v_p), seq_p(d_p), seq_p(a_p), tuple(row(x) for x in par),
                     jnp.zeros((B, H, hd, hd), F32), WKV_CHUNK, T)

    lanes_b = lambda t: t.reshape(DB, S, D).transpose(1, 2, 0)
    par_b = tuple(jnp.broadcast_to(x.reshape(D, 1).astype(F32), (D, DB)) for x in par)
    y_s, st_s = _wkv_sample(lanes_b(r_s), lanes_b(k_s), lanes_b(v_s), lanes_b(d_s), lanes_b(a_s), par_b,
                            jnp.transpose(state_wkv[l].astype(F32), (1, 2, 3, 0)))
    y_s = y_s.transpose(2, 0, 1).reshape(DB * S, D)
    st_s = jnp.transpose(st_s, (3, 0, 1, 2))

    wo_r = b16(rwkv_w_o[l])
    hp = _gated_out(y_p.reshape(B * Tp, D), g_p, hp, wo_r)
    hs = _gated_out(y_s, g_s, hs, wo_r)

    wup1, wdn1 = b16(ffn_w_up[1]), b16(ffn_w_down[1])
    _, yp = _ffn(hp, row(norm_ffn[1]), wup1, wdn1, row(norm_final))
    _, ys = _ffn(hs, row(norm_ffn[1]), wup1, wdn1, row(norm_final))

    y_prompt = yp.reshape(B, Tp, D)[:, n_meta:T]
    y_sample = ys.reshape(DB, S, D)
    k_rope_p = jnp.swapaxes(kpeT_p[:, :, :T], 1, 2)
    k_rope_s = kpeT_s[0].reshape(rope, DB, S).transpose(1, 2, 0)
    return (y_prompt, y_sample,
            c_p[None, :, :T], k_rope_p[None],
            c_s.reshape(1, DB, S, kv_lora), k_rope_s[None],
            st_p[None], np3[None, :, T - 1], st_s[None], ns3[None, :, S - 1])
```

```python
import functools
import math

import jax
import jax.numpy as jnp
from jax import lax
from jax.experimental import pallas as pl
from jax.experimental.pallas import tpu as pltpu

F32 = jnp.float32
BF16 = jnp.bfloat16

RMS_EPS = 1e-6
GN_EPS = 64e-5
NEG_INF = -1e30
ROPE_THETA = 10000.0
LOG2E = 1.4426950408889634

LANES = 128
VMEM_LIMIT_BYTES = 56 * 1024 * 1024

ATTN_Q_TOKENS = 256
ATTN_K_SMALL = 256
ATTN_K_BIG = 512
ATTN_SUB_ROWS = 256
SEQ_ALIGN = 256
ROW_TILE = 768
RWKV_ROW_TILE = 384
FF_TILE = 1024
WKV_CHUNK = 64
PAGES_PER_GROUP = 16
SAMPLE_SLOTS = 3


def _cparams(*sem):
    return pltpu.CompilerParams(dimension_semantics=sem, vmem_limit_bytes=VMEM_LIMIT_BYTES)


def _dot(a, b):
    return jnp.dot(a, b, preferred_element_type=F32)


def _dot_nt(a, b):
    return lax.dot_general(a, b, (((1,), (1,)), ((), ())), preferred_element_type=F32)


def _dot_tn(a, b):
    return lax.dot_general(a, b, (((0,), (0,)), ((), ())), preferred_element_type=F32)


def _rms(x, g):
    return x * lax.rsqrt(jnp.mean(x * x, axis=-1, keepdims=True) + RMS_EPS) * g


def _row_tile(n, target=ROW_TILE, align=8):
    best = None
    for t in range(align, min(n, target) + 1, align):
        if n % t == 0:
            best = t
    assert best is not None, n
    return best


def _mla_proj_kernel(x_ref, g_ref, wa_ref, wkvT_ref, wkT_ref, qn_ref, kvn_ref, kvnc_ref, wnope_ref, wpe_ref,
                     wpesw_ref, wuk_ref, cosq_ref, sinq_ref, cosk_ref, sink_ref,
                     q_ref, ckv_ref, kpeT_ref, ckvb_ref, kT_ref,
                     *, q_lora, kv_lora, rope, heads, nope, scale):
    tm = x_ref.shape[1]
    qk_w = kv_lora + LANES
    n = _rms(x_ref[0], g_ref[...]).astype(BF16)
    a = _dot(n, wa_ref[...])
    c_q = _rms(a[:, :q_lora], qn_ref[...]).astype(BF16)
    c_kv = _rms(a[:, q_lora:], kvn_ref[...])
    ckv_ref[0] = c_kv
    ckvb_ref[0] = c_kv.astype(BF16)
    a_kv_t = _dot_nt(wkvT_ref[...], n)
    c_kv_t = a_kv_t * lax.rsqrt(jnp.mean(a_kv_t * a_kv_t, axis=0, keepdims=True) + RMS_EPS) * kvnc_ref[...]
    a_t = _dot_nt(wkT_ref[...], n)
    k_pe_t = a_t[:rope] * cosk_ref[...] + a_t[rope:] * sink_ref[...]
    kpeT_ref[0] = k_pe_t
    kT_ref[0] = jnp.concatenate([c_kv_t.astype(BF16), k_pe_t.astype(BF16),
                                 jnp.zeros((LANES - rope, tm), BF16)], axis=0)
    cosq = jnp.tile(cosq_ref[...], (1, heads))
    sinq = jnp.tile(sinq_ref[...], (1, heads))
    q_pe = _dot(c_q, wpe_ref[...]) * cosq + _dot(c_q, wpesw_ref[...]) * sinq
    q_pe = (q_pe * scale).astype(BF16)
    q_nope = _dot(c_q, wnope_ref[...]).astype(BF16)
    for h in range(heads):
        q_lat = _dot(q_nope[:, h * nope:(h + 1) * nope], wuk_ref[h])
        q_ref[0, :, h * qk_w:h * qk_w + kv_lora] = (q_lat * scale).astype(BF16)
        q_ref[0, :, h * qk_w + kv_lora:(h + 1) * qk_w] = q_pe[:, h * LANES:(h + 1) * LANES]


def _mla_project(x, g, wts, tabs, dims):
    B, T, D = x.shape
    q_lora, kv_lora, rope, heads, nope, scale = dims
    qk_w = kv_lora + LANES
    tm = _row_tile(T, align=LANES)
    wa, wkvT, wkT, qn, kvn, kvnc, wnope, wpe, wpesw, wuk = wts
    cosq, sinq, cosk, sink = tabs
    full = lambda arr: pl.BlockSpec(arr.shape, lambda b, i: (0,) * arr.ndim)
    row = lambda w: pl.BlockSpec((1, tm, w), lambda b, i: (b, i, 0))
    col = lambda r: pl.BlockSpec((1, r, tm), lambda b, i: (b, 0, i))
    kern = functools.partial(_mla_proj_kernel, q_lora=q_lora, kv_lora=kv_lora, rope=rope,
                             heads=heads, nope=nope, scale=scale)
    return pl.pallas_call(
        kern,
        grid=(B, T // tm),
        in_specs=[row(D), full(g), full(wa), full(wkvT), full(wkT), full(qn), full(kvn), full(kvnc), full(wnope),
                  full(wpe), full(wpesw), full(wuk),
                  pl.BlockSpec((tm, LANES), lambda b, i: (i, 0)),
                  pl.BlockSpec((tm, LANES), lambda b, i: (i, 0)),
                  pl.BlockSpec((rope, tm), lambda b, i: (0, i)),
                  pl.BlockSpec((rope, tm), lambda b, i: (0, i))],
        out_specs=[row(heads * qk_w), row(kv_lora), col(rope), row(kv_lora), col(qk_w)],
        out_shape=[jax.ShapeDtypeStruct((B, T, heads * qk_w), BF16),
                   jax.ShapeDtypeStruct((B, T, kv_lora), F32),
                   jax.ShapeDtypeStruct((B, rope, T), F32),
                   jax.ShapeDtypeStruct((B, T, kv_lora), BF16),
                   jax.ShapeDtypeStruct((B, qk_w, T), BF16)],
        compiler_params=_cparams("parallel", "parallel"),
        name="mla_project",
    )(x, g, wa, wkvT, wkT, qn, kvn, kvnc, wnope, wpe, wpesw, wuk, cosq, sinq, cosk, sink)


def _softmax_update(s, vals, m_scr, l_scr, acc_scr, rows=slice(None)):
    m_prev = m_scr[rows, :]
    m_next = jnp.maximum(m_prev, jnp.max(s, axis=1, keepdims=True))
    p = jnp.exp2(s - jnp.tile(m_next, (1, s.shape[1] // LANES)))
    alpha = jnp.exp2(m_prev - m_next)
    l_scr[rows, :] = alpha * l_scr[rows, :] + jnp.sum(p, axis=1, keepdims=True)
    acc_scr[rows, :] = (acc_scr[rows, :] * jnp.tile(alpha, (1, acc_scr.shape[1] // LANES))
                        + _dot(p.astype(BF16), vals))
    m_scr[rows, :] = m_next


def _attn_prompt_kernel(q_ref, c_ref, kT_ref, o_ref, q_scr, m_scr, l_scr, acc_scr,
                        *, heads, tq, small, big, kv_lora, sub_rows):
    qi = pl.program_id(1)
    rows = tq * heads
    qk_w = kv_lora + LANES
    for h in range(heads):
        q_scr[h * tq:(h + 1) * tq, :] = q_ref[0, :, h * qk_w:(h + 1) * qk_w]
    m_scr[...] = jnp.full(m_scr.shape, -jnp.inf, F32)
    l_scr[...] = jnp.zeros(l_scr.shape, F32)
    acc_scr[...] = jnp.zeros(acc_scr.shape, F32)
    per_big = big // small
    n_sub = rows // sub_rows

    def chunk(j, n_small, masked):
        tk = n_small * small
        start = pl.multiple_of(j * small, small)
        kc = c_ref[0, pl.ds(start, tk), :]
        kT = jnp.concatenate([kT_ref[0, j + i] for i in range(n_small)], axis=1)
        score = lambda i: _dot(q_scr[i * sub_rows:(i + 1) * sub_rows, :], kT)
        s_next = score(0)
        for i in range(n_sub):
            s = s_next
            if i + 1 < n_sub:
                s_next = score(i + 1)
            if masked:
                row = i * sub_rows + lax.broadcasted_iota(jnp.int32, (sub_rows, tk), 0)
                key = start + lax.broadcasted_iota(jnp.int32, (sub_rows, tk), 1)
                s = jnp.where(key <= qi * tq + (row & (tq - 1)), s, NEG_INF)
            _softmax_update(s, kc, m_scr, l_scr, acc_scr, slice(i * sub_rows, (i + 1) * sub_rows))

    n_full = (qi * tq + 1) // small
    n_total = (qi * tq + tq - 1) // small + 1
    n_big = n_full // per_big

    def big_body(j, carry):
        chunk(j * per_big, per_big, False)
        return carry

    def small_body(j, carry):
        chunk(j, 1, False)
        return carry

    def masked_body(j, carry):
        chunk(j, 1, True)
        return carry

    lax.fori_loop(0, n_big, big_body, 0)
    lax.fori_loop(n_big * per_big, n_full, small_body, 0)
    lax.fori_loop(n_full, n_total, masked_body, 0)
    inv = 1.0 / l_scr[...]
    o = acc_scr[...] * jnp.tile(inv, (1, kv_lora // LANES))
    for h in range(heads):
        o_ref[0, :, h * kv_lora:(h + 1) * kv_lora] = o[h * tq:(h + 1) * tq].astype(o_ref.dtype)


def _attn_prompt(q, c_kv, k_t, heads):
    B, T, C = c_kv.shape
    qk_w = C + LANES
    tq, small, big = ATTN_Q_TOKENS, ATTN_K_SMALL, ATTN_K_BIG
    assert T % small == 0 and big % small == 0 and small % tq == 0 and tq & (tq - 1) == 0 and C % LANES == 0
    rows = tq * heads
    kern = functools.partial(_attn_prompt_kernel, heads=heads, tq=tq, small=small, big=big, kv_lora=C,
                             sub_rows=math.gcd(rows, ATTN_SUB_ROWS))
    return pl.pallas_call(
        kern,
        grid=(B, T // tq),
        in_specs=[pl.BlockSpec((1, tq, heads * qk_w), lambda b, i: (b, i, 0)),
                  pl.BlockSpec((1, T, C), lambda b, i: (b, 0, 0)),
                  pl.BlockSpec((1, T // small, qk_w, small), lambda b, i: (b, 0, 0, 0))],
        out_specs=pl.BlockSpec((1, tq, heads * C), lambda b, i: (b, i, 0)),
        out_shape=jax.ShapeDtypeStruct((B, T, heads * C), BF16),
        scratch_shapes=[pltpu.VMEM((rows, qk_w), BF16),
                        pltpu.VMEM((rows, LANES), F32), pltpu.VMEM((rows, LANES), F32),
                        pltpu.VMEM((rows, C), F32)],
        compiler_params=_cparams("parallel", "arbitrary"),
        name="attn_prompt",
    )(q, c_kv, k_t)


def _attn_sample_kernel(pt_ref, ql_ref, qp_ref, cn_ref, knT_ref, cache_c, cache_kT, o_ref,
                        cbuf, kbuf, sem, m_scr, l_scr, acc_scr,
                        *, layer, group, n_groups, total, slots, page, heads):
    b = pl.program_id(0)

    def copies(gid):
        slot = gid % slots
        out = []
        for g in range(group):
            pg = pt_ref[gid * group + g]
            out.append(pltpu.make_async_copy(cache_c.at[layer, pg], cbuf.at[slot, pl.ds(g * page, page), :],
                                             sem.at[0, slot]))
            out.append(pltpu.make_async_copy(cache_kT.at[layer, pg], kbuf.at[slot, :, pl.ds(g * page, page)],
                                             sem.at[1, slot]))
        return out

    @pl.when(b == 0)
    def _():
        for g0 in range(min(slots - 1, total)):
            for cp in copies(g0):
                cp.start()

    m_scr[...] = jnp.full(m_scr.shape, -jnp.inf, F32)
    l_scr[...] = jnp.zeros(l_scr.shape, F32)
    acc_scr[...] = jnp.zeros(acc_scr.shape, F32)
    ql = ql_ref[0]
    qp = qp_ref[0]

    def body(j, carry):
        gid = b * n_groups + j

        @pl.when(gid + slots - 1 < total)
        def _():
            for cp in copies(gid + slots - 1):
                cp.start()

        for cp in copies(gid):
            cp.wait()
        slot = gid % slots
        kc = cbuf[slot].astype(BF16)
        kT = kbuf[slot].astype(BF16)
        _softmax_update(_dot_nt(ql, kc) + _dot(qp, kT), kc, m_scr, l_scr, acc_scr)
        return carry

    lax.fori_loop(0, n_groups, body, 0)

    cn = cn_ref[0]
    s = _dot_nt(ql, cn) + _dot(qp, knT_ref[0])
    rows = s.shape[0]
    tok = lax.broadcasted_iota(jnp.int32, (rows, LANES), 0) // heads
    key = lax.broadcasted_iota(jnp.int32, (rows, LANES), 1)
    _softmax_update(jnp.where(key <= tok, s, NEG_INF), cn, m_scr, l_scr, acc_scr)
    inv = 1.0 / l_scr[...]
    o_ref[0] = (acc_scr[...] * jnp.tile(inv, (1, acc_scr.shape[1] // LANES))).astype(o_ref.dtype)


def _attn_sample(q_lat, q_pe, c_new, k_new_t, cache_c, cache_kT, page_table, layer, heads):
    DB, rows, C = q_lat.shape
    R = cache_kT.shape[2]
    page = cache_c.shape[2]
    n_pages = page_table.shape[1]
    assert page == LANES and rows // heads <= LANES
    group = math.gcd(PAGES_PER_GROUP, n_pages)
    n_groups = n_pages // group
    pt = page_table.reshape(-1).astype(jnp.int32)
    per_b = lambda shape: pl.BlockSpec((1,) + shape, lambda b, pt_ref: (b, 0, 0))
    slots = SAMPLE_SLOTS
    kern = functools.partial(_attn_sample_kernel, layer=layer, group=group, n_groups=n_groups,
                             total=DB * n_groups, slots=slots, page=page, heads=heads)
    grid_spec = pltpu.PrefetchScalarGridSpec(
        num_scalar_prefetch=1,
        grid=(DB,),
        in_specs=[per_b((rows, C)), per_b((rows, R)), per_b((LANES, C)), per_b((R, LANES)),
                  pl.BlockSpec(memory_space=pl.ANY), pl.BlockSpec(memory_space=pl.ANY)],
        out_specs=per_b((rows, C)),
        scratch_shapes=[pltpu.VMEM((slots, group * page, C), F32), pltpu.VMEM((slots, R, group * page), F32),
                        pltpu.SemaphoreType.DMA((2, slots)),
                        pltpu.VMEM((rows, LANES), F32), pltpu.VMEM((rows, LANES), F32),
                        pltpu.VMEM((rows, C), F32)],
    )
    return pl.pallas_call(
        kern,
        grid_spec=grid_spec,
        out_shape=jax.ShapeDtypeStruct((DB, rows, C), BF16),
        compiler_params=_cparams("arbitrary"),
        name="attn_sample",
    )(pt, q_lat, q_pe, c_new, k_new_t, cache_c, cache_kT)


def _mla_out_kernel(o_ref, x_ref, wuv_ref, wo_ref, h_ref, cat_scr, *, heads, kv_lora, vhead):
    for h in range(heads):
        oh = _dot(o_ref[:, h * kv_lora:(h + 1) * kv_lora], wuv_ref[h])
        cat_scr[:, h * vhead:(h + 1) * vhead] = oh.astype(BF16)
    h_ref[...] = x_ref[...] + _dot(cat_scr[...], wo_ref[...])


def _mla_out(o_lat, x, wuv, wo):
    N, D = x.shape
    heads, kv_lora, vhead = wuv.shape
    tm = _row_tile(N)
    kern = functools.partial(_mla_out_kernel, heads=heads, kv_lora=kv_lora, vhead=vhead)
    return pl.pallas_call(
        kern,
        grid=(N // tm,),
        in_specs=[pl.BlockSpec((tm, heads * kv_lora), lambda i: (i, 0)),
                  pl.BlockSpec((tm, D), lambda i: (i, 0)),
                  pl.BlockSpec(wuv.shape, lambda i: (0, 0, 0)),
                  pl.BlockSpec(wo.shape, lambda i: (0, 0))],
        out_specs=pl.BlockSpec((tm, D), lambda i: (i, 0)),
        out_shape=jax.ShapeDtypeStruct((N, D), F32),
        scratch_shapes=[pltpu.VMEM((tm, heads * vhead), BF16)],
        compiler_params=_cparams("parallel"),
        name="mla_out",
    )(o_lat, x, wuv, wo)


def _ffn_kernel(h_ref, g_ref, wup_ref, wdn_ref, gnext_ref, hout_ref, nnext_ref, xn_scr, acc_scr):
    k = pl.program_id(1)

    @pl.when(k == 0)
    def _():
        xn_scr[...] = _rms(h_ref[...], g_ref[...]).astype(BF16)
        acc_scr[...] = jnp.zeros(acc_scr.shape, F32)

    u = jnp.maximum(_dot(xn_scr[...], wup_ref[...]), 0.0)
    acc_scr[...] += _dot((u * u).astype(BF16), wdn_ref[...])

    @pl.when(k == pl.num_programs(1) - 1)
    def _():
        out = h_ref[...] + acc_scr[...]
        hout_ref[...] = out
        nnext_ref[...] = _rms(out, gnext_ref[...])


def _ffn(h, g, wup, wdn, g_next):
    N, D = h.shape
    FF = wup.shape[1]
    tm = _row_tile(N)
    tf = min(FF_TILE, FF)
    return pl.pallas_call(
        _ffn_kernel,
        grid=(N // tm, FF // tf),
        in_specs=[pl.BlockSpec((tm, D), lambda i, k: (i, 0)),
                  pl.BlockSpec((1, D), lambda i, k: (0, 0)),
                  pl.BlockSpec((D, tf), lambda i, k: (0, k)),
                  pl.BlockSpec((tf, D), lambda i, k: (k, 0)),
                  pl.BlockSpec((1, D), lambda i, k: (0, 0))],
        out_specs=[pl.BlockSpec((tm, D), lambda i, k: (i, 0)),
                   pl.BlockSpec((tm, D), lambda i, k: (i, 0))],
        out_shape=[jax.ShapeDtypeStruct((N, D), F32), jax.ShapeDtypeStruct((N, D), F32)],
        scratch_shapes=[pltpu.VMEM((tm, D), BF16), pltpu.VMEM((tm, D), F32)],
        compiler_params=_cparams("parallel", "arbitrary"),
        name="ffn",
    )(h, g, wup, wdn, g_next)


def _rwkv_proj_kernel(n_ref, xp_ref, mu_ref, wr_ref, wk_ref, wv_ref, dw0_ref, dw1_ref, dw2_ref,
                      aw0_ref, aw1_ref, aw2_ref, gw1_ref, gw2_ref,
                      r_ref, k_ref, v_ref, d_ref, a_ref, g_ref, *, seq_len):
    n = n_ref[...]
    if seq_len is None:
        x_prev = xp_ref[...]
    else:
        tm = n.shape[0]
        inside = ((pl.program_id(0) * tm) % seq_len != 0).astype(F32)
        row = lax.broadcasted_iota(jnp.int32, (tm, 1), 0)
        x_prev = jnp.where(row == 0, xp_ref[7:8, :] * inside, pltpu.roll(n, 1, axis=0))
    xx = x_prev - n
    mix = lambda j: (n + xx * mu_ref[j:j + 1, :]).astype(BF16)
    r_ref[...] = _dot(mix(0), wr_ref[...])
    k_ref[...] = _dot(mix(2), wk_ref[...])
    v_ref[...] = _dot(mix(3), wv_ref[...])
    z = dw0_ref[...] + _dot(jnp.tanh(_dot(mix(1), dw1_ref[...])).astype(BF16), dw2_ref[...])
    d_ref[...] = (-math.exp(-0.5)) / (1.0 + jnp.exp(-z))
    za = aw0_ref[...] + _dot(_dot(mix(4), aw1_ref[...]).astype(BF16), aw2_ref[...])
    a_ref[...] = 1.0 / (1.0 + jnp.exp(-za))
    zg = _dot(mix(5), gw1_ref[...])
    g_ref[...] = _dot((1.0 / (1.0 + jnp.exp(-zg))).astype(BF16), gw2_ref[...])


def _rwkv_project(n, x_prev, wts, seq_len=None):
    N, D = n.shape
    tm = _row_tile(N if seq_len is None else seq_len, RWKV_ROW_TILE)
    row = pl.BlockSpec((tm, D), lambda i: (i, 0))
    full = lambda arr: pl.BlockSpec(arr.shape, lambda i: (0,) * arr.ndim)
    if seq_len is None:
        prev_spec = row
    else:
        x_prev = n
        prev_spec = pl.BlockSpec((8, D), lambda i: (jnp.maximum(i * (tm // 8) - 1, 0), 0))
    return pl.pallas_call(
        functools.partial(_rwkv_proj_kernel, seq_len=seq_len),
        grid=(N // tm,),
        in_specs=[row, prev_spec] + [full(w) for w in wts],
        out_specs=[row] * 6,
        out_shape=[jax.ShapeDtypeStruct((N, D), F32)] * 6,
        compiler_params=_cparams("parallel"),
        name="rwkv_project",
    )(n, x_prev, *wts)


def _wkv_kernel(r_ref, k_ref, v_ref, d_ref, a_ref, kk_ref, ka_ref, rk_ref, lnw_ref, lnb_ref, s0_ref,
                y_ref, sout_ref, st_scr, *, L, t_valid, nb, pairs, hd):
    c = pl.program_id(0)
    L2 = 2 * L
    W = 2 * hd
    P = range(nb * pairs)
    bi = [q // pairs for q in P]
    hp = [q % pairs for q in P]
    m0 = lax.broadcasted_iota(jnp.int32, (1, W), 1) < hd

    @pl.when(c == 0)
    def _():
        z = jnp.zeros((hd, hd), F32)
        for p in P:
            s_bd = jnp.concatenate([jnp.concatenate([s0_ref[bi[p], 2 * hp[p]], z], axis=1),
                                    jnp.concatenate([z, s0_ref[bi[p], 2 * hp[p] + 1]], axis=1)], axis=0)
            st_scr[p] = s_bd.T

    valid = (c * L + lax.broadcasted_iota(jnp.int32, (L, 1), 0)) < t_valid
    tri = (lax.broadcasted_iota(jnp.int32, (L, L), 0) >= lax.broadcasted_iota(jnp.int32, (L, L), 1))
    tri3 = jnp.tile(tri.astype(BF16), (1, 3))
    ones3 = jnp.ones((3 * L, W), BF16)
    i2 = lax.broadcasted_iota(jnp.int32, (L2, L2), 0)
    j2 = lax.broadcasted_iota(jnp.int32, (L2, L2), 1)
    mask_s = j2 < i2
    mask_i = j2 <= i2
    eye = (i2 == j2).astype(F32)

    def stack(x):
        return jnp.concatenate([jnp.where(m0, x, 0.0), jnp.where(m0, 0.0, x)], axis=0).astype(BF16)

    def head_sum(x):
        s_a = jnp.sum(jnp.where(m0, x, 0.0), axis=1, keepdims=True)
        s_b = jnp.sum(jnp.where(m0, 0.0, x), axis=1, keepdims=True)
        return jnp.where(m0, s_a, s_b)

    def split3(x):
        hi = x.astype(BF16)
        r1 = x - hi.astype(F32)
        mid = r1.astype(BF16)
        lo = (r1 - mid.astype(F32)).astype(BF16)
        return jnp.concatenate([hi, mid, lo], axis=0)

    sl = [slice(hp[p] * W, (hp[p] + 1) * W) for p in P]
    load = lambda ref, p: jnp.where(valid, ref[bi[p], :, sl[p]], 0.0)
    r = [load(r_ref, p) for p in P]
    k = [load(k_ref, p) for p in P]
    v = [load(v_ref, p) for p in P]
    d = [load(d_ref, p) for p in P]
    a = [load(a_ref, p) for p in P]
    d3 = [split3(d[p]) for p in P]
    cum = [_dot(tri3, d3[p]) for p in P]
    kk = [k[p] * kk_ref[:, sl[p]] for p in P]
    kk = [kk[p] / jnp.maximum(jnp.sqrt(head_sum(kk[p] * kk[p])), 1e-12) for p in P]
    kp = [k[p] * (1.0 + (a[p] - 1.0) * ka_ref[:, sl[p]]) for p in P]
    bv = [kk[p] * a[p] for p in P]
    e_neg = [jnp.exp(-cum[p]) for p in P]
    a_st = [stack(-kk[p] * jnp.exp(cum[p] - d[p])) for p in P]
    r_st = [stack(r[p] * jnp.exp(cum[p])) for p in P]
    b_st = [stack(bv[p] * e_neg[p]) for p in P]
    k_st = [stack(kp[p] * e_neg[p]) for p in P]
    v_st = [stack(v[p]) for p in P]
    pm = [_dot_nt(jnp.concatenate([a_st[p], r_st[p]], axis=0), jnp.concatenate([b_st[p], k_st[p]], axis=0))
          for p in P]
    m_ab = [jnp.where(mask_s, pm[p][:L2, :L2], 0.0) for p in P]
    m_ak = [jnp.where(mask_s, pm[p][:L2, L2:], 0.0).astype(BF16) for p in P]
    m_rb = [jnp.where(mask_i, pm[p][L2:, :L2], 0.0).astype(BF16) for p in P]
    m_rk = [jnp.where(mask_i, pm[p][L2:, L2:], 0.0).astype(BF16) for p in P]
    st = [st_scr[p] for p in P]
    stb = [st[p].astype(BF16) for p in P]
    rhs = [_dot(jnp.concatenate([a_st[p], m_ak[p]], axis=1), jnp.concatenate([stb[p], v_st[p]], axis=0)) for p in P]
    inv = [eye + m_ab[p] for p in P]
    pw = [m_ab[p].astype(BF16) for p in P]
    pw = [_dot(pw[p], pw[p]).astype(BF16) for p in P]
    levels = int(math.log2(L))
    for lvl in range(1, levels):
        last = lvl == levels - 1
        z = [_dot(pw[p], inv[p].astype(BF16) if last else jnp.concatenate([inv[p].astype(BF16), pw[p]], axis=1))
             for p in P]
        inv = [inv[p] + z[p][:, :L2] for p in P]
        if not last:
            pw = [z[p][:, L2:].astype(BF16) for p in P]
    u_st = [_dot(inv[p].astype(BF16), rhs[p].astype(BF16)).astype(BF16) for p in P]
    y_st = [_dot(jnp.concatenate([r_st[p], m_rb[p], m_rk[p]], axis=1),
                 jnp.concatenate([stb[p], u_st[p], v_st[p]], axis=0)) for p in P]
    decay_col = [jnp.exp(_dot_tn(d3[p], ones3)) for p in P]
    e_rest = [jnp.exp(cum[p][L - 1:L, :] - cum[p]) for p in P]
    for p in P:
        st_scr[p] = decay_col[p] * st[p] + _dot_tn(
            jnp.concatenate([stack(bv[p] * e_rest[p]), stack(kp[p] * e_rest[p])], axis=0),
            jnp.concatenate([u_st[p], v_st[p]], axis=0))
    for p in P:
        y = y_st[p][:L] + y_st[p][L:]
        mean = head_sum(y) * (1.0 / hd)
        yc = y - mean
        var = head_sum(yc * yc) * (1.0 / hd)
        yn = yc * lax.rsqrt(var + GN_EPS) * lnw_ref[:, sl[p]] + lnb_ref[:, sl[p]]
        y_ref[bi[p], :, sl[p]] = yn + head_sum(r[p] * kp[p] * rk_ref[:, sl[p]]) * v[p]

    @pl.when(c == pl.num_programs(0) - 1)
    def _():
        for p in P:
            s_bd = st_scr[p].T
            sout_ref[bi[p], 2 * hp[p]] = s_bd[:hd, :hd]
            sout_ref[bi[p], 2 * hp[p] + 1] = s_bd[hd:, hd:]


def _wkv(r, k, v, d, a, params, s0, L, t_valid):
    B, T, D = r.shape
    H, hd = s0.shape[1], s0.shape[2]
    assert 2 * hd == LANES and H % 2 == 0 and T % L == 0 and (2 * L) % LANES == 0
    pairs = H // 2
    seq = pl.BlockSpec((B, L, D), lambda c: (0, c, 0))
    vec = pl.BlockSpec((1, D), lambda c: (0, 0))
    state = pl.BlockSpec((B, H, hd, hd), lambda c: (0, 0, 0, 0))
    kern = functools.partial(_wkv_kernel, L=L, t_valid=t_valid, nb=B, pairs=pairs, hd=hd)
    return pl.pallas_call(
        kern,
        grid=(T // L,),
        in_specs=[seq] * 5 + [vec] * 5 + [state],
        out_specs=[seq, state],
        out_shape=[jax.ShapeDtypeStruct((B, T, D), F32), jax.ShapeDtypeStruct(s0.shape, F32)],
        scratch_shapes=[pltpu.VMEM((B * pairs, LANES, LANES), F32)],
        compiler_params=_cparams("arbitrary"),
        name="wkv",
    )(r, k, v, d, a, *params, s0)


def _wkv_sample_kernel(r_ref, k_ref, v_ref, d_ref, a_ref, kk_ref, ka_ref, rk_ref, lnw_ref, lnb_ref, s0_ref,
                       y_ref, sout_ref, w_scr, a_scr, b_scr, kp_scr, yraw_scr, *, steps, hd):
    for t in range(steps):
        kt = k_ref[t]
        at = a_ref[t]
        kk = kt * kk_ref[...]
        kk = kk / jnp.maximum(jnp.sqrt(jnp.sum(kk * kk, axis=0, keepdims=True)), 1e-12)
        w_scr[t] = jnp.exp(d_ref[t])
        a_scr[t] = -kk
        b_scr[t] = kk * at
        kp_scr[t] = kt * (1.0 + (at - 1.0) * ka_ref[...])

    def body(vi, carry):
        s = s0_ref[vi]
        for t in range(steps):
            sa = jnp.sum(s * a_scr[t], axis=0, keepdims=True)
            s = s * w_scr[t] + sa * b_scr[t] + v_ref[t, pl.ds(vi, 1), :] * kp_scr[t]
            yraw_scr[t, pl.ds(vi, 1), :] = jnp.sum(s * r_ref[t], axis=0, keepdims=True)
        sout_ref[vi] = s
        return carry

    lax.fori_loop(0, hd, body, 0)
    for t in range(steps):
        y = yraw_scr[t]
        mean = jnp.mean(y, axis=0, keepdims=True)
        yc = y - mean
        var = jnp.mean(yc * yc, axis=0, keepdims=True)
        yn = yc * lax.rsqrt(var + GN_EPS) * lnw_ref[...] + lnb_ref[...]
        bonus = jnp.sum(r_ref[t] * kp_scr[t] * rk_ref[...], axis=0, keepdims=True)
        y_ref[t] = yn + bonus * v_ref[t]


def _wkv_sample(r, k, v, d, a, params, s0):
    S, D, DB = r.shape
    H, hd = s0.shape[0], s0.shape[1]
    seq = pl.BlockSpec((S, hd, DB), lambda h: (0, h, 0))
    vec = pl.BlockSpec((hd, DB), lambda h: (h, 0))
    state = pl.BlockSpec((None, hd, hd, DB), lambda h: (h, 0, 0, 0))
    kern = functools.partial(_wkv_sample_kernel, steps=S, hd=hd)
    return pl.pallas_call(
        kern,
        grid=(H,),
        in_specs=[seq] * 5 + [vec] * 5 + [state],
        out_specs=[seq, state],
        out_shape=[jax.ShapeDtypeStruct((S, D, DB), F32), jax.ShapeDtypeStruct(s0.shape, F32)],
        scratch_shapes=[pltpu.VMEM((S, hd, DB), F32)] * 5,
        compiler_params=_cparams("parallel"),
        name="wkv_sample",
    )(r, k, v, d, a, *params, s0)


def _gated_out_kernel(y_ref, g_ref, h_ref, wo_ref, o_ref):
    o_ref[...] = h_ref[...] + _dot((y_ref[...] * g_ref[...]).astype(BF16), wo_ref[...])


def _gated_out(y, g, h, wo):
    N, D = h.shape
    tm = _row_tile(N)
    row = pl.BlockSpec((tm, D), lambda i: (i, 0))
    return pl.pallas_call(
        _gated_out_kernel,
        grid=(N // tm,),
        in_specs=[row, row, row, pl.BlockSpec(wo.shape, lambda i: (0, 0))],
        out_specs=row,
        out_shape=jax.ShapeDtypeStruct((N, D), F32),
        compiler_params=_cparams("parallel"),
        name="rwkv_out",
    )(y, g, h, wo)


def _rope_tables(pos, rope, heads):
    half = rope // 2
    inv_freq = ROPE_THETA ** (-jnp.arange(half, dtype=F32) / half)
    ang = pos.astype(F32)[:, None] * inv_freq[None, :]
    cos, sin = jnp.cos(ang), jnp.sin(ang)
    cos2 = jnp.concatenate([cos, cos], axis=-1)
    sin2 = jnp.concatenate([-sin, sin], axis=-1)
    pad = lambda t: jnp.pad(t, ((0, 0), (0, LANES - rope)))
    return pad(cos2), pad(sin2), cos2.T, sin2.T


def _swap_halves(w, width):
    lead = w.shape[:-1]
    g = w.reshape(lead + (-1, 2, width // 2))
    return g[..., ::-1, :].reshape(w.shape)


def _pad_heads(w, heads, rope):
    K = w.shape[0]
    return jnp.pad(w.reshape(K, heads, rope), ((0, 0), (0, 0), (0, LANES - rope))).reshape(K, heads * LANES)


def kernel(x_prompt, x_sample, cache_kv_latent, cache_k_rope, state_wkv, state_shift, page_table, meta_tokens, norm_mix, norm_ffn, norm_final, mla_w_qkv_a, mla_q_a_norm, mla_kv_a_norm, mla_w_q_b, mla_w_kv_b, mla_w_o, rwkv_mu, rwkv_w_r, rwkv_w_k, rwkv_w_v, rwkv_w_o, rwkv_decay_w0, rwkv_decay_w1, rwkv_decay_w2, rwkv_a_w0, rwkv_a_w1, rwkv_a_w2, rwkv_g_w1, rwkv_g_w2, rwkv_k_k, rwkv_k_a, rwkv_r_k, rwkv_ln_w, rwkv_ln_b, ffn_w_up, ffn_w_down):
    B, SEQ, D = x_prompt.shape
    DB, S, _ = x_sample.shape
    n_meta = meta_tokens.shape[0]
    T = n_meta + SEQ
    Tp = -(-T // SEQ_ALIGN) * SEQ_ALIGN
    page = cache_kv_latent.shape[2]
    past_len = page_table.shape[1] * page
    kv_lora = cache_kv_latent.shape[-1]
    rope = cache_k_rope.shape[-1]
    q_lora = mla_q_a_norm.shape[-1]
    H, hd = state_wkv.shape[2], state_wkv.shape[3]
    qk = mla_w_q_b.shape[-1]
    kvb = mla_w_kv_b.shape[-1]
    ov = mla_w_o.shape[1]
    heads = (qk + ov - kvb) // rope
    nope = qk // heads - rope
    vhead = ov // heads
    qk_w = kv_lora + LANES
    scale = float(nope + rope) ** -0.5 * LOG2E
    dims = (q_lora, kv_lora, rope, heads, nope, scale)

    row = lambda vec: vec.reshape(1, -1).astype(F32)
    b16 = lambda w: w.astype(BF16)

    meta = jnp.broadcast_to(meta_tokens.astype(F32)[None], (B, n_meta, D))
    hp = jnp.concatenate([meta, x_prompt, jnp.zeros((B, Tp - T, D), F32)], axis=1)
    hs = x_sample.reshape(1, DB * S, D)

    l = 0
    wqkv = mla_w_qkv_a[l]
    wa = b16(wqkv[:, :q_lora + kv_lora])
    wkvT = b16(wqkv[:, q_lora:q_lora + kv_lora].T)
    wk = wqkv[:, q_lora + kv_lora:]
    wkT = b16(jnp.concatenate([wk, _swap_halves(wk, rope)], axis=1).T)
    wqb = mla_w_q_b[l].reshape(q_lora, heads, nope + rope)
    wnope = b16(wqb[..., :nope].reshape(q_lora, heads * nope))
    wpe = wqb[..., nope:].reshape(q_lora, heads * rope)
    wpesw = b16(_pad_heads(_swap_halves(wpe, rope), heads, rope))
    wpe = b16(_pad_heads(wpe, heads, rope))
    wkvb = mla_w_kv_b[l].reshape(kv_lora, heads, nope + vhead)
    wuk = b16(jnp.transpose(wkvb[..., :nope], (1, 2, 0)))
    wuv = b16(jnp.transpose(wkvb[..., nope:], (1, 0, 2)))
    wo = b16(mla_w_o[l])
    kvn = mla_kv_a_norm[l].astype(F32)
    mla_w = (wa, wkvT, wkT, row(mla_q_a_norm[l]), row(kvn), kvn.reshape(-1, 1), wnope, wpe, wpesw, wuk)

    tabs_p = _rope_tables(jnp.arange(Tp), rope, heads)
    tabs_s = _rope_tables(jnp.tile(past_len + jnp.arange(S), DB), rope, heads)

    q_p, c_p, kpeT_p, cb_p, kT_p = _mla_project(hp, row(norm_mix[0]), mla_w, tabs_p, dims)
    q_s, c_s, kpeT_s, cb_s, kT_s = _mla_project(hs, row(norm_mix[0]), mla_w, tabs_s, dims)

    ks = ATTN_K_SMALL
    kT_chunks = kT_p.reshape(B, qk_w, Tp // ks, ks).transpose(0, 2, 1, 3)
    o_p = _attn_prompt(q_p, cb_p, kT_chunks, heads)

    q_s = q_s.reshape(DB, S * heads, qk_w)
    c_new = jnp.pad(cb_s.reshape(DB, S, kv_lora), ((0, 0), (0, LANES - S), (0, 0)))
    k_new_t = kT_s[0, kv_lora:kv_lora + rope].reshape(rope, DB, S).transpose(1, 0, 2)
    k_new_t = jnp.pad(k_new_t, ((0, 0), (0, 0), (0, LANES - S)))
    o_s = _attn_sample(q_s[..., :kv_lora], q_s[..., kv_lora:kv_lora + rope], c_new, k_new_t,
                       cache_kv_latent, jnp.swapaxes(cache_k_rope, 2, 3), page_table, l, heads)

    hp = _mla_out(o_p.reshape(B * Tp, heads * kv_lora), hp.reshape(B * Tp, D), wuv, wo)
    hs = _mla_out(o_s.reshape(DB * S, heads * kv_lora), hs.reshape(DB * S, D), wuv, wo)

    wup0, wdn0 = b16(ffn_w_up[0]), b16(ffn_w_down[0])
    hp, np_ = _ffn(hp, row(norm_ffn[0]), wup0, wdn0, row(norm_mix[1]))
    hs, ns_ = _ffn(hs, row(norm_ffn[0]), wup0, wdn0, row(norm_mix[1]))

    np3 = np_.reshape(B, Tp, D)
    ns3 = ns_.reshape(DB, S, D)
    xprev_s = jnp.concatenate([state_shift[l].astype(F32)[:, None], ns3[:, :-1]], axis=1).reshape(DB * S, D)
    rw = (rwkv_mu[l].astype(F32), b16(rwkv_w_r[l]), b16(rwkv_w_k[l]), b16(rwkv_w_v[l]),
          row(rwkv_decay_w0[l]), b16(rwkv_decay_w1[l]), b16(rwkv_decay_w2[l]),
          row(rwkv_a_w0[l]), b16(rwkv_a_w1[l]), b16(rwkv_a_w2[l]),
          b16(rwkv_g_w1[l]), b16(rwkv_g_w2[l]))
    par = (rwkv_k_k[l], rwkv_k_a[l], rwkv_r_k[l], rwkv_ln_w[l], rwkv_ln_b[l])

    r_p, k_p, v_p, d_p, a_p, g_p = _rwkv_project(np_, None, rw, seq_len=Tp)
    r_s, k_s, v_s, d_s, a_s, g_s = _rwkv_project(ns_, xprev_s, rw)

    seq_p = lambda t: t.reshape(B, Tp, D)
    y_p, st_p = _wkv(seq_p(r_p), seq_p(k_p), seq_p(v_p), seq_p(d_p), seq_p(a_p), tuple(row(x) for x in par),
                     jnp.zeros((B, H, hd, hd), F32), WKV_CHUNK, T)

    lanes_b = lambda t: t.reshape(DB, S, D).transpose(1, 2, 0)
    par_b = tuple(jnp.broadcast_to(x.reshape(D, 1).astype(F32), (D, DB)) for x in par)
    y_s, st_s = _wkv_sample(lanes_b(r_s), lanes_b(k_s), lanes_b(v_s), lanes_b(d_s), lanes_b(a_s), par_b,
                            jnp.transpose(state_wkv[l].astype(F32), (1, 2, 3, 0)))
    y_s = y_s.transpose(2, 0, 1).reshape(DB * S, D)
    st_s = jnp.transpose(st_s, (3, 0, 1, 2))

    wo_r = b16(rwkv_w_o[l])
    hp = _gated_out(y_p.reshape(B * Tp, D), g_p, hp, wo_r)
    hs = _gated_out(y_s, g_s, hs, wo_r)

    wup1, wdn1 = b16(ffn_w_up[1]), b16(ffn_w_down[1])
    _, yp = _ffn(hp, row(norm_ffn[1]), wup1, wdn1, row(norm_final))
    _, ys = _ffn(hs, row(norm_ffn[1]), wup1, wdn1, row(norm_final))

    y_prompt = yp.reshape(B, Tp, D)[:, n_meta:T]
    y_sample = ys.reshape(DB, S, D)
    k_rope_p = jnp.swapaxes(kpeT_p[:, :, :T], 1, 2)
    k_rope_s = kpeT_s[0].reshape(rope, DB, S).transpose(1, 2, 0)
    return (y_prompt, y_sample,
            c_p[None, :, :T], k_rope_p[None],
            c_s.reshape(1, DB, S, kv_lora), k_rope_s[None],
            st_p[None], np3[None, :, T - 1], st_s[None], ns3[None, :, S - 1])
```

```python
import functools
import math

import jax
import jax.numpy as jnp
from jax import lax
from jax.experimental import pallas as pl
from jax.experimental.pallas import tpu as pltpu

F32 = jnp.float32
BF16 = jnp.bfloat16

RMS_EPS = 1e-6
GN_EPS = 64e-5
NEG_INF = -1e30
ROPE_THETA = 10000.0
LOG2E = 1.4426950408889634

LANES = 128
VMEM_LIMIT_BYTES = 56 * 1024 * 1024

ATTN_Q_TOKENS = 256
ATTN_K_SMALL = 256
ATTN_K_BIG = 512
ATTN_SUB_ROWS = 256
SEQ_ALIGN = 256
ROW_TILE = 768
RWKV_ROW_TILE = 384
FF_TILE = 1024
FFN_WINDOW_TILE = 1024
WKV_CHUNK = 64
WKV_SAMPLE_ROWS = 2
PAGES_PER_GROUP = 16
SAMPLE_SLOTS = 3
SAMPLE_PARTS = 4


def _cparams(*sem):
    return pltpu.CompilerParams(dimension_semantics=sem, vmem_limit_bytes=VMEM_LIMIT_BYTES)


def _dot(a, b):
    return jnp.dot(a, b, preferred_element_type=F32)


def _dot_nt(a, b):
    return lax.dot_general(a, b, (((1,), (1,)), ((), ())), preferred_element_type=F32)


def _dot_tn(a, b):
    return lax.dot_general(a, b, (((0,), (0,)), ((), ())), preferred_element_type=F32)


def _rms(x, g):
    return x * lax.rsqrt(jnp.mean(x * x, axis=-1, keepdims=True) + RMS_EPS) * g


def _row_tile(n, target=ROW_TILE, align=8):
    best = None
    for t in range(align, min(n, target) + 1, align):
        if n % t == 0:
            best = t
    assert best is not None, n
    return best


def _mla_proj_kernel(x_ref, g_ref, wa_ref, wkvT_ref, wkT_ref, qn_ref, kvn_ref, kvnc_ref, wnope_ref, wpe_ref,
                     wpesw_ref, wuk_ref, cosq_ref, sinq_ref, cosk_ref, sink_ref,
                     q_ref, ckv_ref, kpeT_ref, ckvb_ref, kT_ref,
                     *, q_lora, kv_lora, rope, heads, nope, scale):
    tm = x_ref.shape[1]
    qk_w = kv_lora + LANES
    n = _rms(x_ref[0], g_ref[...]).astype(BF16)
    a = _dot(n, wa_ref[...])
    c_q = _rms(a[:, :q_lora], qn_ref[...]).astype(BF16)
    c_kv = _rms(a[:, q_lora:], kvn_ref[...])
    ckv_ref[0] = c_kv
    ckvb_ref[0] = c_kv.astype(BF16)
    a_kv_t = _dot_nt(wkvT_ref[...], n)
    c_kv_t = a_kv_t * lax.rsqrt(jnp.mean(a_kv_t * a_kv_t, axis=0, keepdims=True) + RMS_EPS) * kvnc_ref[...]
    a_t = _dot_nt(wkT_ref[...], n)
    k_pe_t = a_t[:rope] * cosk_ref[...] + a_t[rope:] * sink_ref[...]
    kpeT_ref[0] = k_pe_t
    kT_ref[0] = jnp.concatenate([c_kv_t.astype(BF16), k_pe_t.astype(BF16),
                                 jnp.zeros((LANES - rope, tm), BF16)], axis=0)
    cosq = jnp.tile(cosq_ref[...], (1, heads))
    sinq = jnp.tile(sinq_ref[...], (1, heads))
    q_pe = _dot(c_q, wpe_ref[...]) * cosq + _dot(c_q, wpesw_ref[...]) * sinq
    q_pe = (q_pe * scale).astype(BF16)
    q_nope = _dot(c_q, wnope_ref[...]).astype(BF16)
    for h in range(heads):
        q_lat = _dot(q_nope[:, h * nope:(h + 1) * nope], wuk_ref[h])
        q_ref[0, :, h * qk_w:h * qk_w + kv_lora] = (q_lat * scale).astype(BF16)
        q_ref[0, :, h * qk_w + kv_lora:(h + 1) * qk_w] = q_pe[:, h * LANES:(h + 1) * LANES]


def _mla_project(x, g, wts, tabs, dims):
    B, T, D = x.shape
    q_lora, kv_lora, rope, heads, nope, scale = dims
    qk_w = kv_lora + LANES
    tm = _row_tile(T, align=LANES)
    wa, wkvT, wkT, qn, kvn, kvnc, wnope, wpe, wpesw, wuk = wts
    cosq, sinq, cosk, sink = tabs
    full = lambda arr: pl.BlockSpec(arr.shape, lambda b, i: (0,) * arr.ndim)
    row = lambda w: pl.BlockSpec((1, tm, w), lambda b, i: (b, i, 0))
    col = lambda r: pl.BlockSpec((1, r, tm), lambda b, i: (b, 0, i))
    kern = functools.partial(_mla_proj_kernel, q_lora=q_lora, kv_lora=kv_lora, rope=rope,
                             heads=heads, nope=nope, scale=scale)
    return pl.pallas_call(
        kern,
        grid=(B, T // tm),
        in_specs=[row(D), full(g), full(wa), full(wkvT), full(wkT), full(qn), full(kvn), full(kvnc), full(wnope),
                  full(wpe), full(wpesw), full(wuk),
                  pl.BlockSpec((tm, LANES), lambda b, i: (i, 0)),
                  pl.BlockSpec((tm, LANES), lambda b, i: (i, 0)),
                  pl.BlockSpec((rope, tm), lambda b, i: (0, i)),
                  pl.BlockSpec((rope, tm), lambda b, i: (0, i))],
        out_specs=[row(heads * qk_w), row(kv_lora), col(rope), row(kv_lora), col(qk_w)],
        out_shape=[jax.ShapeDtypeStruct((B, T, heads * qk_w), BF16),
                   jax.ShapeDtypeStruct((B, T, kv_lora), F32),
                   jax.ShapeDtypeStruct((B, rope, T), F32),
                   jax.ShapeDtypeStruct((B, T, kv_lora), BF16),
                   jax.ShapeDtypeStruct((B, qk_w, T), BF16)],
        compiler_params=_cparams("parallel", "parallel"),
        name="mla_project",
    )(x, g, wa, wkvT, wkT, qn, kvn, kvnc, wnope, wpe, wpesw, wuk, cosq, sinq, cosk, sink)


def _softmax_update(s, vals, m_scr, l_scr, acc_scr, rows=slice(None)):
    m_prev = m_scr[rows, :]
    m_next = jnp.maximum(m_prev, jnp.max(s, axis=1, keepdims=True))
    p = jnp.exp2(s - jnp.tile(m_next, (1, s.shape[1] // LANES)))
    alpha = jnp.exp2(m_prev - m_next)
    l_scr[rows, :] = alpha * l_scr[rows, :] + jnp.sum(p, axis=1, keepdims=True)
    acc_scr[rows, :] = (acc_scr[rows, :] * jnp.tile(alpha, (1, acc_scr.shape[1] // LANES))
                        + _dot(p.astype(BF16), vals))
    m_scr[rows, :] = m_next


def _attn_prompt_kernel(q_ref, c_ref, kT_ref, o_ref, q_scr, m_scr, l_scr, acc_scr,
                        *, heads, tq, small, big, kv_lora, sub_rows):
    qi = pl.program_id(1)
    rows = tq * heads
    qk_w = kv_lora + LANES
    for h in range(heads):
        q_scr[h * tq:(h + 1) * tq, :] = q_ref[0, :, h * qk_w:(h + 1) * qk_w]
    m_scr[...] = jnp.full(m_scr.shape, -jnp.inf, F32)
    l_scr[...] = jnp.zeros(l_scr.shape, F32)
    acc_scr[...] = jnp.zeros(acc_scr.shape, F32)
    per_big = big // small
    n_sub = rows // sub_rows

    def chunk(j, n_small, masked):
        tk = n_small * small
        start = pl.multiple_of(j * small, small)
        kc = c_ref[0, pl.ds(start, tk), :]
        kT = jnp.concatenate([kT_ref[0, j + i] for i in range(n_small)], axis=1)
        score = lambda i: _dot(q_scr[i * sub_rows:(i + 1) * sub_rows, :], kT)
        s_next = score(0)
        for i in range(n_sub):
            s = s_next
            if i + 1 < n_sub:
                s_next = score(i + 1)
            if masked:
                row = i * sub_rows + lax.broadcasted_iota(jnp.int32, (sub_rows, tk), 0)
                key = start + lax.broadcasted_iota(jnp.int32, (sub_rows, tk), 1)
                s = jnp.where(key <= qi * tq + (row & (tq - 1)), s, NEG_INF)
            _softmax_update(s, kc, m_scr, l_scr, acc_scr, slice(i * sub_rows, (i + 1) * sub_rows))

    n_full = (qi * tq + 1) // small
    n_total = (qi * tq + tq - 1) // small + 1
    n_big = n_full // per_big

    def big_body(j, carry):
        chunk(j * per_big, per_big, False)
        return carry

    lax.fori_loop(0, n_big, big_body, 0)
    n_rest = n_total - n_big * per_big
    for size in range(1, per_big + 1):
        def rest_body(j, carry, size=size):
            chunk(n_big * per_big, size, True)
            return carry

        lax.fori_loop(0, (n_rest == size).astype(jnp.int32), rest_body, 0)
    inv = 1.0 / l_scr[...]
    o = acc_scr[...] * jnp.tile(inv, (1, kv_lora // LANES))
    for h in range(heads):
        o_ref[0, :, h * kv_lora:(h + 1) * kv_lora] = o[h * tq:(h + 1) * tq].astype(o_ref.dtype)


def _attn_prompt(q, c_kv, k_t, heads):
    B, T, C = c_kv.shape
    qk_w = C + LANES
    tq, small, big = ATTN_Q_TOKENS, ATTN_K_SMALL, ATTN_K_BIG
    assert T % small == 0 and big % small == 0 and small % tq == 0 and tq & (tq - 1) == 0 and C % LANES == 0
    rows = tq * heads
    kern = functools.partial(_attn_prompt_kernel, heads=heads, tq=tq, small=small, big=big, kv_lora=C,
                             sub_rows=math.gcd(rows, ATTN_SUB_ROWS))
    return pl.pallas_call(
        kern,
        grid=(B, T // tq),
        in_specs=[pl.BlockSpec((1, tq, heads * qk_w), lambda b, i: (b, i, 0)),
                  pl.BlockSpec((1, T, C), lambda b, i: (b, 0, 0)),
                  pl.BlockSpec((1, T // small, qk_w, small), lambda b, i: (b, 0, 0, 0))],
        out_specs=pl.BlockSpec((1, tq, heads * C), lambda b, i: (b, i, 0)),
        out_shape=jax.ShapeDtypeStruct((B, T, heads * C), BF16),
        scratch_shapes=[pltpu.VMEM((rows, qk_w), BF16),
                        pltpu.VMEM((rows, LANES), F32), pltpu.VMEM((rows, LANES), F32),
                        pltpu.VMEM((rows, C), F32)],
        compiler_params=_cparams("parallel", "arbitrary"),
        name="attn_prompt",
    )(q, c_kv, k_t)


def _attn_sample_kernel(pt_ref, ql_ref, qp_ref, cn_ref, knT_ref, cache_c, cache_kT, o_ref,
                        cbuf, kbuf, sem, m_scr, l_scr, acc_scr,
                        *, layer, group, n_groups, total, slots, parts, page, heads):
    b = pl.program_id(0)

    def copies(gid):
        slot = gid % slots
        out = []
        for g in range(group):
            pg = pt_ref[gid * group + g]
            out.append(pltpu.make_async_copy(cache_c.at[layer, pg], cbuf.at[slot, pl.ds(g * page, page), :],
                                             sem.at[0, slot]))
            out.append(pltpu.make_async_copy(cache_kT.at[layer, pg], kbuf.at[slot, :, pl.ds(g * page, page)],
                                             sem.at[1, slot]))
        return out

    @pl.when(b == 0)
    def _():
        for g0 in range(min(slots - 1, total)):
            for cp in copies(g0):
                cp.start()

    m_scr[...] = jnp.full(m_scr.shape, -jnp.inf, F32)
    l_scr[...] = jnp.zeros(l_scr.shape, F32)
    acc_scr[...] = jnp.zeros(acc_scr.shape, F32)
    ql = ql_ref[0]
    qp = qp_ref[0]

    def body(j, carry):
        gid = b * n_groups + j

        @pl.when(gid + slots - 1 < total)
        def _():
            for cp in copies(gid + slots - 1):
                cp.start()

        for cp in copies(gid):
            cp.wait()
        slot = gid % slots
        part = group * page // parts
        kcs = [cbuf[slot, i * part:(i + 1) * part, :].astype(BF16) for i in range(parts)]
        kTs = [kbuf[slot, :, i * part:(i + 1) * part].astype(BF16) for i in range(parts)]
        scores = [_dot_nt(ql, kcs[i]) + _dot(qp, kTs[i]) for i in range(parts)]
        for i in range(parts):
            _softmax_update(scores[i], kcs[i], m_scr, l_scr, acc_scr)
        return carry

    lax.fori_loop(0, n_groups, body, 0)

    cn = cn_ref[0]
    s = _dot_nt(ql, cn) + _dot(qp, knT_ref[0])
    rows = s.shape[0]
    tok = lax.broadcasted_iota(jnp.int32, (rows, LANES), 0) // heads
    key = lax.broadcasted_iota(jnp.int32, (rows, LANES), 1)
    _softmax_update(jnp.where(key <= tok, s, NEG_INF), cn, m_scr, l_scr, acc_scr)
    inv = 1.0 / l_scr[...]
    o_ref[0] = (acc_scr[...] * jnp.tile(inv, (1, acc_scr.shape[1] // LANES))).astype(o_ref.dtype)


def _attn_sample(q_lat, q_pe, c_new, k_new_t, cache_c, cache_kT, page_table, layer, heads):
    DB, rows, C = q_lat.shape
    R = cache_kT.shape[2]
    page = cache_c.shape[2]
    n_pages = page_table.shape[1]
    assert page == LANES and rows // heads <= LANES
    group = math.gcd(PAGES_PER_GROUP, n_pages)
    n_groups = n_pages // group
    pt = page_table.reshape(-1).astype(jnp.int32)
    per_b = lambda shape: pl.BlockSpec((1,) + shape, lambda b, pt_ref: (b, 0, 0))
    slots = SAMPLE_SLOTS
    kern = functools.partial(_attn_sample_kernel, layer=layer, group=group, n_groups=n_groups,
                             total=DB * n_groups, slots=slots, parts=math.gcd(SAMPLE_PARTS, group), page=page,
                             heads=heads)
    grid_spec = pltpu.PrefetchScalarGridSpec(
        num_scalar_prefetch=1,
        grid=(DB,),
        in_specs=[per_b((rows, C)), per_b((rows, R)), per_b((LANES, C)), per_b((R, LANES)),
                  pl.BlockSpec(memory_space=pl.ANY), pl.BlockSpec(memory_space=pl.ANY)],
        out_specs=per_b((rows, C)),
        scratch_shapes=[pltpu.VMEM((slots, group * page, C), F32), pltpu.VMEM((slots, R, group * page), F32),
                        pltpu.SemaphoreType.DMA((2, slots)),
                        pltpu.VMEM((rows, LANES), F32), pltpu.VMEM((rows, LANES), F32),
                        pltpu.VMEM((rows, C), F32)],
    )
    return pl.pallas_call(
        kern,
        grid_spec=grid_spec,
        out_shape=jax.ShapeDtypeStruct((DB, rows, C), BF16),
        compiler_params=_cparams("arbitrary"),
        name="attn_sample",
    )(pt, q_lat, q_pe, c_new, k_new_t, cache_c, cache_kT)


def _mla_out_kernel(o_ref, x_ref, wuv_ref, wo_ref, h_ref, cat_scr, *, heads, kv_lora, vhead):
    for h in range(heads):
        oh = _dot(o_ref[:, h * kv_lora:(h + 1) * kv_lora], wuv_ref[h])
        cat_scr[:, h * vhead:(h + 1) * vhead] = oh.astype(BF16)
    h_ref[...] = x_ref[...] + _dot(cat_scr[...], wo_ref[...])


def _mla_out(o_lat, x, wuv, wo):
    N, D = x.shape
    heads, kv_lora, vhead = wuv.shape
    tm = _row_tile(N)
    kern = functools.partial(_mla_out_kernel, heads=heads, kv_lora=kv_lora, vhead=vhead)
    return pl.pallas_call(
        kern,
        grid=(N // tm,),
        in_specs=[pl.BlockSpec((tm, heads * kv_lora), lambda i: (i, 0)),
                  pl.BlockSpec((tm, D), lambda i: (i, 0)),
                  pl.BlockSpec(wuv.shape, lambda i: (0, 0, 0)),
                  pl.BlockSpec(wo.shape, lambda i: (0, 0))],
        out_specs=pl.BlockSpec((tm, D), lambda i: (i, 0)),
        out_shape=jax.ShapeDtypeStruct((N, D), F32),
        scratch_shapes=[pltpu.VMEM((tm, heads * vhead), BF16)],
        compiler_params=_cparams("parallel"),
        name="mla_out",
    )(o_lat, x, wuv, wo)


def _ffn_kernel(h_ref, g_ref, wup_ref, wdn_ref, gnext_ref, *rest, emit_h):
    if emit_h:
        hout_ref, nnext_ref, xn_scr, acc_scr = rest
    else:
        nnext_ref, xn_scr, acc_scr = rest
    k = pl.program_id(2)

    @pl.when(k == 0)
    def _():
        xn_scr[...] = _rms(h_ref[...], g_ref[...]).astype(BF16)
        acc_scr[...] = jnp.zeros(acc_scr.shape, F32)

    u = jnp.maximum(_dot(xn_scr[...], wup_ref[...]), 0.0)
    acc_scr[...] += _dot((u * u).astype(BF16), wdn_ref[...])

    @pl.when(k == pl.num_programs(2) - 1)
    def _():
        out = h_ref[...] + acc_scr[...]
        if emit_h:
            hout_ref[...] = out
        nnext_ref[...] = _rms(out, gnext_ref[...])


def _ffn(h, g, wup, wdn, g_next, *, emit_h=True, window=None):
    N, D = h.shape
    FF = wup.shape[1]
    stride, offset, length = (N, 0, N) if window is None else window
    n_seq = N // stride
    tf = min(FF_TILE, FF)
    if offset == 0 and length == stride:
        tm = _row_tile(length)
        x_spec = pl.BlockSpec((tm, D), lambda b, i, k: (b * (stride // tm) + i, 0))
    else:
        tm = _row_tile(length, FFN_WINDOW_TILE)
        assert stride % 8 == 0 and offset % 8 == 0 and tm % 8 == 0
        x_spec = pl.BlockSpec((pl.Element(tm), pl.Element(D)),
                              lambda b, i, k: (pl.multiple_of(b * stride + offset + i * tm, 8), 0))
    out_spec = pl.BlockSpec((tm, D), lambda b, i, k: (b * (length // tm) + i, 0))
    out_sds = jax.ShapeDtypeStruct((n_seq * length, D), F32)
    n_out = 2 if emit_h else 1
    res = pl.pallas_call(
        functools.partial(_ffn_kernel, emit_h=emit_h),
        grid=(n_seq, length // tm, FF // tf),
        in_specs=[x_spec,
                  pl.BlockSpec((1, D), lambda b, i, k: (0, 0)),
                  pl.BlockSpec((D, tf), lambda b, i, k: (0, k)),
                  pl.BlockSpec((tf, D), lambda b, i, k: (k, 0)),
                  pl.BlockSpec((1, D), lambda b, i, k: (0, 0))],
        out_specs=[out_spec] * n_out,
        out_shape=[out_sds] * n_out,
        scratch_shapes=[pltpu.VMEM((tm, D), BF16), pltpu.VMEM((tm, D), F32)],
        compiler_params=_cparams("parallel", "parallel", "arbitrary"),
        name="ffn",
    )(h, g, wup, wdn, g_next)
    return res if emit_h else (None, res[0])


def _rwkv_proj_kernel(n_ref, xp_ref, mu_ref, wr_ref, wk_ref, wv_ref, dw0_ref, dw1_ref, dw2_ref,
                      aw0_ref, aw1_ref, aw2_ref, gw1_ref, gw2_ref,
                      r_ref, k_ref, v_ref, d_ref, a_ref, g_ref, *, seq_len):
    n = n_ref[...]
    if seq_len is None:
        x_prev = xp_ref[...]
    else:
        tm = n.shape[0]
        inside = ((pl.program_id(0) * tm) % seq_len != 0).astype(F32)
        row = lax.broadcasted_iota(jnp.int32, (tm, 1), 0)
        x_prev = jnp.where(row == 0, xp_ref[7:8, :] * inside, pltpu.roll(n, 1, axis=0))
    xx = x_prev - n
    mix = lambda j: (n + xx * mu_ref[j:j + 1, :]).astype(BF16)
    r_ref[...] = _dot(mix(0), wr_ref[...])
    k_ref[...] = _dot(mix(2), wk_ref[...])
    v_ref[...] = _dot(mix(3), wv_ref[...])
    z = dw0_ref[...] + _dot(jnp.tanh(_dot(mix(1), dw1_ref[...])).astype(BF16), dw2_ref[...])
    d_ref[...] = (-math.exp(-0.5)) / (1.0 + jnp.exp(-z))
    za = aw0_ref[...] + _dot(_dot(mix(4), aw1_ref[...]).astype(BF16), aw2_ref[...])
    a_ref[...] = 1.0 / (1.0 + jnp.exp(-za))
    zg = _dot(mix(5), gw1_ref[...])
    g_ref[...] = _dot((1.0 / (1.0 + jnp.exp(-zg))).astype(BF16), gw2_ref[...])


def _rwkv_project(n, x_prev, wts, seq_len=None):
    N, D = n.shape
    tm = _row_tile(N if seq_len is None else seq_len, RWKV_ROW_TILE)
    row = pl.BlockSpec((tm, D), lambda i: (i, 0))
    full = lambda arr: pl.BlockSpec(arr.shape, lambda i: (0,) * arr.ndim)
    if seq_len is None:
        prev_spec = row
    else:
        x_prev = n
        prev_spec = pl.BlockSpec((8, D), lambda i: (jnp.maximum(i * (tm // 8) - 1, 0), 0))
    return pl.pallas_call(
        functools.partial(_rwkv_proj_kernel, seq_len=seq_len),
        grid=(N // tm,),
        in_specs=[row, prev_spec] + [full(w) for w in wts],
        out_specs=[row] * 6,
        out_shape=[jax.ShapeDtypeStruct((N, D), F32)] * 6,
        compiler_params=_cparams("parallel"),
        name="rwkv_project",
    )(n, x_prev, *wts)


def _wkv_kernel(r_ref, k_ref, v_ref, d_ref, a_ref, kk_ref, ka_ref, rk_ref, lnw_ref, lnb_ref, s0_ref,
                y_ref, sout_ref, st_scr, *, L, t_valid, nb, pairs, hd):
    c = pl.program_id(0)
    L2 = 2 * L
    W = 2 * hd
    P = range(nb * pairs)
    bi = [q // pairs for q in P]
    hp = [q % pairs for q in P]
    m0 = lax.broadcasted_iota(jnp.int32, (1, W), 1) < hd

    @pl.when(c == 0)
    def _():
        z = jnp.zeros((hd, hd), F32)
        for p in P:
            s_bd = jnp.concatenate([jnp.concatenate([s0_ref[bi[p], 2 * hp[p]], z], axis=1),
                                    jnp.concatenate([z, s0_ref[bi[p], 2 * hp[p] + 1]], axis=1)], axis=0)
            st_scr[p] = s_bd.T

    valid = (c * L + lax.broadcasted_iota(jnp.int32, (L, 1), 0)) < t_valid
    tri = (lax.broadcasted_iota(jnp.int32, (L, L), 0) >= lax.broadcasted_iota(jnp.int32, (L, L), 1))
    tri3 = jnp.tile(tri.astype(BF16), (1, 3))
    ones3 = jnp.ones((3 * L, W), BF16)
    i2 = lax.broadcasted_iota(jnp.int32, (L2, L2), 0)
    j2 = lax.broadcasted_iota(jnp.int32, (L2, L2), 1)
    mask_s = j2 < i2
    mask_i = j2 <= i2
    eye = (i2 == j2).astype(F32)

    def stack(x):
        return jnp.concatenate([jnp.where(m0, x, 0.0), jnp.where(m0, 0.0, x)], axis=0).astype(BF16)

    def head_sum(x):
        s_a = jnp.sum(jnp.where(m0, x, 0.0), axis=1, keepdims=True)
        s_b = jnp.sum(jnp.where(m0, 0.0, x), axis=1, keepdims=True)
        return jnp.where(m0, s_a, s_b)

    def split3(x):
        hi = x.astype(BF16)
        r1 = x - hi.astype(F32)
        mid = r1.astype(BF16)
        lo = (r1 - mid.astype(F32)).astype(BF16)
        return jnp.concatenate([hi, mid, lo], axis=0)

    sl = [slice(hp[p] * W, (hp[p] + 1) * W) for p in P]
    load = lambda ref, p: jnp.where(valid, ref[bi[p], :, sl[p]], 0.0)
    r = [load(r_ref, p) for p in P]
    k = [load(k_ref, p) for p in P]
    v = [load(v_ref, p) for p in P]
    d = [load(d_ref, p) for p in P]
    a = [load(a_ref, p) for p in P]
    d3 = [split3(d[p]) for p in P]
    cum = [_dot(tri3, d3[p]) for p in P]
    kk = [k[p] * kk_ref[:, sl[p]] for p in P]
    kk = [kk[p] / jnp.maximum(jnp.sqrt(head_sum(kk[p] * kk[p])), 1e-12) for p in P]
    kp = [k[p] * (1.0 + (a[p] - 1.0) * ka_ref[:, sl[p]]) for p in P]
    bv = [kk[p] * a[p] for p in P]
    e_neg = [jnp.exp(-cum[p]) for p in P]
    a_st = [stack(-kk[p] * jnp.exp(cum[p] - d[p])) for p in P]
    r_st = [stack(r[p] * jnp.exp(cum[p])) for p in P]
    b_st = [stack(bv[p] * e_neg[p]) for p in P]
    k_st = [stack(kp[p] * e_neg[p]) for p in P]
    v_st = [stack(v[p]) for p in P]
    pm = [_dot_nt(jnp.concatenate([a_st[p], r_st[p]], axis=0), jnp.concatenate([b_st[p], k_st[p]], axis=0))
          for p in P]
    m_ab = [jnp.where(mask_s, pm[p][:L2, :L2], 0.0) for p in P]
    m_ak = [jnp.where(mask_s, pm[p][:L2, L2:], 0.0).astype(BF16) for p in P]
    m_rb = [jnp.where(mask_i, pm[p][L2:, :L2], 0.0).astype(BF16) for p in P]
    m_rk = [jnp.where(mask_i, pm[p][L2:, L2:], 0.0).astype(BF16) for p in P]
    st = [st_scr[p] for p in P]
    stb = [st[p].astype(BF16) for p in P]
    rhs = [_dot(jnp.concatenate([a_st[p], m_ak[p]], axis=1), jnp.concatenate([stb[p], v_st[p]], axis=0)) for p in P]
    inv = [eye + m_ab[p] for p in P]
    pw = [m_ab[p].astype(BF16) for p in P]
    pw = [_dot(pw[p], pw[p]).astype(BF16) for p in P]
    levels = int(math.log2(L))
    for lvl in range(1, levels):
        last = lvl == levels - 1
        z = [_dot(pw[p], inv[p].astype(BF16) if last else jnp.concatenate([inv[p].astype(BF16), pw[p]], axis=1))
             for p in P]
        inv = [inv[p] + z[p][:, :L2] for p in P]
        if not last:
            pw = [z[p][:, L2:].astype(BF16) for p in P]
    u_st = [_dot(inv[p].astype(BF16), rhs[p].astype(BF16)).astype(BF16) for p in P]
    y_st = [_dot(jnp.concatenate([r_st[p], m_rb[p], m_rk[p]], axis=1),
                 jnp.concatenate([stb[p], u_st[p], v_st[p]], axis=0)) for p in P]
    decay_col = [jnp.exp(_dot_tn(d3[p], ones3)) for p in P]
    e_rest = [jnp.exp(cum[p][L - 1:L, :] - cum[p]) for p in P]
    for p in P:
        st_scr[p] = decay_col[p] * st[p] + _dot_tn(
            jnp.concatenate([stack(bv[p] * e_rest[p]), stack(kp[p] * e_rest[p])], axis=0),
            jnp.concatenate([u_st[p], v_st[p]], axis=0))
    for p in P:
        y = y_st[p][:L] + y_st[p][L:]
        mean = head_sum(y) * (1.0 / hd)
        yc = y - mean
        var = head_sum(yc * yc) * (1.0 / hd)
        yn = yc * lax.rsqrt(var + GN_EPS) * lnw_ref[:, sl[p]] + lnb_ref[:, sl[p]]
        y_ref[bi[p], :, sl[p]] = yn + head_sum(r[p] * kp[p] * rk_ref[:, sl[p]]) * v[p]

    @pl.when(c == pl.num_programs(0) - 1)
    def _():
        for p in P:
            s_bd = st_scr[p].T
            sout_ref[bi[p], 2 * hp[p]] = s_bd[:hd, :hd]
            sout_ref[bi[p], 2 * hp[p] + 1] = s_bd[hd:, hd:]


def _wkv(r, k, v, d, a, params, s0, L, t_valid):
    B, T, D = r.shape
    H, hd = s0.shape[1], s0.shape[2]
    assert 2 * hd == LANES and H % 2 == 0 and T % L == 0 and (2 * L) % LANES == 0
    pairs = H // 2
    seq = pl.BlockSpec((B, L, D), lambda c: (0, c, 0))
    vec = pl.BlockSpec((1, D), lambda c: (0, 0))
    state = pl.BlockSpec((B, H, hd, hd), lambda c: (0, 0, 0, 0))
    kern = functools.partial(_wkv_kernel, L=L, t_valid=t_valid, nb=B, pairs=pairs, hd=hd)
    return pl.pallas_call(
        kern,
        grid=(T // L,),
        in_specs=[seq] * 5 + [vec] * 5 + [state],
        out_specs=[seq, state],
        out_shape=[jax.ShapeDtypeStruct((B, T, D), F32), jax.ShapeDtypeStruct(s0.shape, F32)],
        scratch_shapes=[pltpu.VMEM((B * pairs, LANES, LANES), F32)],
        compiler_params=_cparams("arbitrary"),
        name="wkv",
    )(r, k, v, d, a, *params, s0)


def _wkv_sample_kernel(r_ref, k_ref, v_ref, d_ref, a_ref, kk_ref, ka_ref, rk_ref, lnw_ref, lnb_ref, s0_ref,
                       y_ref, sout_ref, w_scr, a_scr, b_scr, kp_scr, yraw_scr, *, steps, hd):
    for t in range(steps):
        kt = k_ref[t]
        at = a_ref[t]
        kk = kt * kk_ref[...]
        kk = kk / jnp.maximum(jnp.sqrt(jnp.sum(kk * kk, axis=0, keepdims=True)), 1e-12)
        w_scr[t] = jnp.exp(d_ref[t])
        a_scr[t] = -kk
        b_scr[t] = kk * at
        kp_scr[t] = kt * (1.0 + (at - 1.0) * ka_ref[...])

    def body(i, carry):
        rows = [i * WKV_SAMPLE_ROWS + j for j in range(WKV_SAMPLE_ROWS)]
        s = [s0_ref[vi] for vi in rows]
        for t in range(steps):
            sa = [jnp.sum(s[j] * a_scr[t], axis=0, keepdims=True) for j in range(len(rows))]
            s = [s[j] * w_scr[t] + sa[j] * b_scr[t] + v_ref[t, pl.ds(vi, 1), :] * kp_scr[t]
                 for j, vi in enumerate(rows)]
            for j, vi in enumerate(rows):
                yraw_scr[t, pl.ds(vi, 1), :] = jnp.sum(s[j] * r_ref[t], axis=0, keepdims=True)
        for j, vi in enumerate(rows):
            sout_ref[vi] = s[j]
        return carry

    assert hd % WKV_SAMPLE_ROWS == 0
    lax.fori_loop(0, hd // WKV_SAMPLE_ROWS, body, 0)
    for t in range(steps):
        y = yraw_scr[t]
        mean = jnp.mean(y, axis=0, keepdims=True)
        yc = y - mean
        var = jnp.mean(yc * yc, axis=0, keepdims=True)
        yn = yc * lax.rsqrt(var + GN_EPS) * lnw_ref[...] + lnb_ref[...]
        bonus = jnp.sum(r_ref[t] * kp_scr[t] * rk_ref[...], axis=0, keepdims=True)
        y_ref[t] = yn + bonus * v_ref[t]


def _wkv_sample(r, k, v, d, a, params, s0):
    S, D, DB = r.shape
    H, hd = s0.shape[0], s0.shape[1]
    seq = pl.BlockSpec((S, hd, DB), lambda h: (0, h, 0))
    vec = pl.BlockSpec((hd, DB), lambda h: (h, 0))
    state = pl.BlockSpec((None, hd, hd, DB), lambda h: (h, 0, 0, 0))
    kern = functools.partial(_wkv_sample_kernel, steps=S, hd=hd)
    return pl.pallas_call(
        kern,
        grid=(H,),
        in_specs=[seq] * 5 + [vec] * 5 + [state],
        out_specs=[seq, state],
        out_shape=[jax.ShapeDtypeStruct((S, D, DB), F32), jax.ShapeDtypeStruct(s0.shape, F32)],
        scratch_shapes=[pltpu.VMEM((S, hd, DB), F32)] * 5,
        compiler_params=_cparams("parallel"),
        name="wkv_sample",
    )(r, k, v, d, a, *params, s0)


def _gated_out_kernel(y_ref, g_ref, h_ref, wo_ref, o_ref):
    o_ref[...] = h_ref[...] + _dot((y_ref[...] * g_ref[...]).astype(BF16), wo_ref[...])


def _gated_out(y, g, h, wo):
    N, D = h.shape
    tm = _row_tile(N)
    row = pl.BlockSpec((tm, D), lambda i: (i, 0))
    return pl.pallas_call(
        _gated_out_kernel,
        grid=(N // tm,),
        in_specs=[row, row, row, pl.BlockSpec(wo.shape, lambda i: (0, 0))],
        out_specs=row,
        out_shape=jax.ShapeDtypeStruct((N, D), F32),
        compiler_params=_cparams("parallel"),
        name="rwkv_out",
    )(y, g, h, wo)


def _rope_tables(pos, rope, heads):
    half = rope // 2
    inv_freq = ROPE_THETA ** (-jnp.arange(half, dtype=F32) / half)
    ang = pos.astype(F32)[:, None] * inv_freq[None, :]
    cos, sin = jnp.cos(ang), jnp.sin(ang)
    cos2 = jnp.concatenate([cos, cos], axis=-1)
    sin2 = jnp.concatenate([-sin, sin], axis=-1)
    pad = lambda t: jnp.pad(t, ((0, 0), (0, LANES - rope)))
    return pad(cos2), pad(sin2), cos2.T, sin2.T


def _swap_halves(w, width):
    lead = w.shape[:-1]
    g = w.reshape(lead + (-1, 2, width // 2))
    return g[..., ::-1, :].reshape(w.shape)


def _pad_heads(w, heads, rope):
    K = w.shape[0]
    return jnp.pad(w.reshape(K, heads, rope), ((0, 0), (0, 0), (0, LANES - rope))).reshape(K, heads * LANES)


def kernel(x_prompt, x_sample, cache_kv_latent, cache_k_rope, state_wkv, state_shift, page_table, meta_tokens, norm_mix, norm_ffn, norm_final, mla_w_qkv_a, mla_q_a_norm, mla_kv_a_norm, mla_w_q_b, mla_w_kv_b, mla_w_o, rwkv_mu, rwkv_w_r, rwkv_w_k, rwkv_w_v, rwkv_w_o, rwkv_decay_w0, rwkv_decay_w1, rwkv_decay_w2, rwkv_a_w0, rwkv_a_w1, rwkv_a_w2, rwkv_g_w1, rwkv_g_w2, rwkv_k_k, rwkv_k_a, rwkv_r_k, rwkv_ln_w, rwkv_ln_b, ffn_w_up, ffn_w_down):
    B, SEQ, D = x_prompt.shape
    DB, S, _ = x_sample.shape
    n_meta = meta_tokens.shape[0]
    T = n_meta + SEQ
    Tp = -(-T // SEQ_ALIGN) * SEQ_ALIGN
    page = cache_kv_latent.shape[2]
    past_len = page_table.shape[1] * page
    kv_lora = cache_kv_latent.shape[-1]
    rope = cache_k_rope.shape[-1]
    q_lora = mla_q_a_norm.shape[-1]
    H, hd = state_wkv.shape[2], state_wkv.shape[3]
    qk = mla_w_q_b.shape[-1]
    kvb = mla_w_kv_b.shape[-1]
    ov = mla_w_o.shape[1]
    heads = (qk + ov - kvb) // rope
    nope = qk // heads - rope
    vhead = ov // heads
    qk_w = kv_lora + LANES
    scale = float(nope + rope) ** -0.5 * LOG2E
    dims = (q_lora, kv_lora, rope, heads, nope, scale)

    row = lambda vec: vec.reshape(1, -1).astype(F32)
    b16 = lambda w: w.astype(BF16)

    meta = jnp.broadcast_to(meta_tokens.astype(F32)[None], (B, n_meta, D))
    hp = jnp.concatenate([meta, x_prompt, jnp.zeros((B, Tp - T, D), F32)], axis=1)
    hs = x_sample.reshape(1, DB * S, D)

    l = 0
    wqkv = mla_w_qkv_a[l]
    wa = b16(wqkv[:, :q_lora + kv_lora])
    wkvT = b16(wqkv[:, q_lora:q_lora + kv_lora].T)
    wk = wqkv[:, q_lora + kv_lora:]
    wkT = b16(jnp.concatenate([wk, _swap_halves(wk, rope)], axis=1).T)
    wqb = mla_w_q_b[l].reshape(q_lora, heads, nope + rope)
    wnope = b16(wqb[..., :nope].reshape(q_lora, heads * nope))
    wpe = wqb[..., nope:].reshape(q_lora, heads * rope)
    wpesw = b16(_pad_heads(_swap_halves(wpe, rope), heads, rope))
    wpe = b16(_pad_heads(wpe, heads, rope))
    wkvb = mla_w_kv_b[l].reshape(kv_lora, heads, nope + vhead)
    wuk = b16(jnp.transpose(wkvb[..., :nope], (1, 2, 0)))
    wuv = b16(jnp.transpose(wkvb[..., nope:], (1, 0, 2)))
    wo = b16(mla_w_o[l])
    kvn = mla_kv_a_norm[l].astype(F32)
    mla_w = (wa, wkvT, wkT, row(mla_q_a_norm[l]), row(kvn), kvn.reshape(-1, 1), wnope, wpe, wpesw, wuk)

    tabs_p = _rope_tables(jnp.arange(Tp), rope, heads)
    tabs_s = _rope_tables(jnp.tile(past_len + jnp.arange(S), DB), rope, heads)

    q_p, c_p, kpeT_p, cb_p, kT_p = _mla_project(hp, row(norm_mix[0]), mla_w, tabs_p, dims)
    q_s, c_s, kpeT_s, cb_s, kT_s = _mla_project(hs, row(norm_mix[0]), mla_w, tabs_s, dims)

    ks = ATTN_K_SMALL
    kT_chunks = kT_p.reshape(B, qk_w, Tp // ks, ks).transpose(0, 2, 1, 3)
    o_p = _attn_prompt(q_p, cb_p, kT_chunks, heads)

    q_s = q_s.reshape(DB, S * heads, qk_w)
    c_new = jnp.pad(cb_s.reshape(DB, S, kv_lora), ((0, 0), (0, LANES - S), (0, 0)))
    k_new_t = kT_s[0, kv_lora:kv_lora + rope].reshape(rope, DB, S).transpose(1, 0, 2)
    k_new_t = jnp.pad(k_new_t, ((0, 0), (0, 0), (0, LANES - S)))
    o_s = _attn_sample(q_s[..., :kv_lora], q_s[..., kv_lora:kv_lora + rope], c_new, k_new_t,
                       cache_kv_latent, jnp.swapaxes(cache_k_rope, 2, 3), page_table, l, heads)

    hp = _mla_out(o_p.reshape(B * Tp, heads * kv_lora), hp.reshape(B * Tp, D), wuv, wo)
    hs = _mla_out(o_s.reshape(DB * S, heads * kv_lora), hs.reshape(DB * S, D), wuv, wo)

    wup0, wdn0 = b16(ffn_w_up[0]), b16(ffn_w_down[0])
    hp, np_ = _ffn(hp, row(norm_ffn[0]), wup0, wdn0, row(norm_mix[1]))
    hs, ns_ = _ffn(hs, row(norm_ffn[0]), wup0, wdn0, row(norm_mix[1]))

    np3 = np_.reshape(B, Tp, D)
    ns3 = ns_.reshape(DB, S, D)
    xprev_s = jnp.concatenate([state_shift[l].astype(F32)[:, None], ns3[:, :-1]], axis=1).reshape(DB * S, D)
    rw = (rwkv_mu[l].astype(F32), b16(rwkv_w_r[l]), b16(rwkv_w_k[l]), b16(rwkv_w_v[l]),
          row(rwkv_decay_w0[l]), b16(rwkv_decay_w1[l]), b16(rwkv_decay_w2[l]),
          row(rwkv_a_w0[l]), b16(rwkv_a_w1[l]), b16(rwkv_a_w2[l]),
          b16(rwkv_g_w1[l]), b16(rwkv_g_w2[l]))
    par = (rwkv_k_k[l], rwkv_k_a[l], rwkv_r_k[l], rwkv_ln_w[l], rwkv_ln_b[l])

    r_p, k_p, v_p, d_p, a_p, g_p = _rwkv_project(np_, None, rw, seq_len=Tp)
    r_s, k_s, v_s, d_s, a_s, g_s = _rwkv_project(ns_, xprev_s, rw)

    seq_p = lambda t: t.reshape(B, Tp, D)
    y_p, st_p = _wkv(seq_p(r_p), seq_p(k_p), seq_p(v_p), seq_p(d_p), seq_p(a_p), tuple(row(x) for x in par),
                     jnp.zeros((B, H, hd, hd), F32), WKV_CHUNK, T)

    lanes_b = lambda t: t.reshape(DB, S, D).transpose(1, 2, 0)
    par_b = tuple(jnp.broadcast_to(x.reshape(D, 1).astype(F32), (D, DB)) for x in par)
    y_s, st_s = _wkv_sample(lanes_b(r_s), lanes_b(k_s), lanes_b(v_s), lanes_b(d_s), lanes_b(a_s), par_b,
                            jnp.transpose(state_wkv[l].astype(F32), (1, 2, 3, 0)))
    y_s = y_s.transpose(2, 0, 1).reshape(DB * S, D)
    st_s = jnp.transpose(st_s, (3, 0, 1, 2))

    wo_r = b16(rwkv_w_o[l])
    hp = _gated_out(y_p.reshape(B * Tp, D), g_p, hp, wo_r)
    hs = _gated_out(y_s, g_s, hs, wo_r)

    wup1, wdn1 = b16(ffn_w_up[1]), b16(ffn_w_down[1])
    _, yp = _ffn(hp, row(norm_ffn[1]), wup1, wdn1, row(norm_final), emit_h=False, window=(Tp, n_meta, SEQ))
    _, ys = _ffn(hs, row(norm_ffn[1]), wup1, wdn1, row(norm_final), emit_h=False)

    y_prompt = yp.reshape(B, SEQ, D)
    y_sample = ys.reshape(DB, S, D)
    k_rope_p = jnp.swapaxes(kpeT_p[:, :, :T], 1, 2)
    k_rope_s = kpeT_s[0].reshape(rope, DB, S).transpose(1, 2, 0)
    return (y_prompt, y_sample,
            c_p[None, :, :T], k_rope_p[None],
            c_s.reshape(1, DB, S, kv_lora), k_rope_s[None],
            st_p[None], np3[None, :, T - 1], st_s[None], ns3[None, :, S - 1])
```

```python
import functools
import math

import jax
import jax.numpy as jnp
from jax import lax
from jax.experimental import pallas as pl
from jax.experimental.pallas import tpu as pltpu

F32 = jnp.float32
BF16 = jnp.bfloat16

RMS_EPS = 1e-6
GN_EPS = 64e-5
NEG_INF = -1e30
ROPE_THETA = 10000.0
LOG2E = 1.4426950408889634

LANES = 128
VMEM_LIMIT_BYTES = 56 * 1024 * 1024

ATTN_Q_TOKENS = 256
ATTN_K_SMALL = 256
ATTN_K_LEVELS = (4, 2)
ATTN_SUB_ROWS = 256
SEQ_ALIGN = 256
ROW_TILE = 768
RWKV_ROW_TILE = 384
FF_TILE = 1024
FFN_WINDOW_TILE = 512
WKV_CHUNK = 64
WKV_SAMPLE_ROWS = 2
PAGES_PER_GROUP = 16
SAMPLE_SLOTS = 3
SAMPLE_PARTS = 4


def _cparams(*sem):
    return pltpu.CompilerParams(dimension_semantics=sem, vmem_limit_bytes=VMEM_LIMIT_BYTES)


def _dot(a, b):
    return jnp.dot(a, b, preferred_element_type=F32)


def _dot_nt(a, b):
    return lax.dot_general(a, b, (((1,), (1,)), ((), ())), preferred_element_type=F32)


def _dot_tn(a, b):
    return lax.dot_general(a, b, (((0,), (0,)), ((), ())), preferred_element_type=F32)


def _rms(x, g):
    return x * lax.rsqrt(jnp.mean(x * x, axis=-1, keepdims=True) + RMS_EPS) * g


def _row_tile(n, target=ROW_TILE, align=8):
    best = None
    for t in range(align, min(n, target) + 1, align):
        if n % t == 0:
            best = t
    assert best is not None, n
    return best


def _mla_proj_kernel(x_ref, g_ref, wa_ref, wkvT_ref, wkT_ref, qn_ref, kvn_ref, kvnc_ref, wnope_ref, wpe_ref,
                     wpesw_ref, wuk_ref, cosq_ref, sinq_ref, cosk_ref, sink_ref,
                     q_ref, ckv_ref, kpeT_ref, ckvb_ref, kT_ref,
                     *, q_lora, kv_lora, rope, heads, nope, scale):
    tm = x_ref.shape[1]
    qk_w = kv_lora + LANES
    n = _rms(x_ref[0], g_ref[...]).astype(BF16)
    a = _dot(n, wa_ref[...])
    c_q = _rms(a[:, :q_lora], qn_ref[...]).astype(BF16)
    c_kv = _rms(a[:, q_lora:], kvn_ref[...])
    ckv_ref[0] = c_kv
    ckvb_ref[0] = c_kv.astype(BF16)
    a_kv_t = _dot_nt(wkvT_ref[...], n)
    c_kv_t = a_kv_t * lax.rsqrt(jnp.mean(a_kv_t * a_kv_t, axis=0, keepdims=True) + RMS_EPS) * kvnc_ref[...]
    a_t = _dot_nt(wkT_ref[...], n)
    k_pe_t = a_t[:rope] * cosk_ref[...] + a_t[rope:] * sink_ref[...]
    kpeT_ref[0] = k_pe_t
    kT_ref[0] = jnp.concatenate([c_kv_t.astype(BF16), k_pe_t.astype(BF16),
                                 jnp.zeros((LANES - rope, tm), BF16)], axis=0)
    cosq = jnp.tile(cosq_ref[...], (1, heads))
    sinq = jnp.tile(sinq_ref[...], (1, heads))
    q_pe = _dot(c_q, wpe_ref[...]) * cosq + _dot(c_q, wpesw_ref[...]) * sinq
    q_pe = (q_pe * scale).astype(BF16)
    q_nope = _dot(c_q, wnope_ref[...]).astype(BF16)
    for h in range(heads):
        q_lat = _dot(q_nope[:, h * nope:(h + 1) * nope], wuk_ref[h])
        q_ref[0, :, h * qk_w:h * qk_w + kv_lora] = (q_lat * scale).astype(BF16)
        q_ref[0, :, h * qk_w + kv_lora:(h + 1) * qk_w] = q_pe[:, h * LANES:(h + 1) * LANES]


def _mla_project(x, g, wts, tabs, dims):
    B, T, D = x.shape
    q_lora, kv_lora, rope, heads, nope, scale = dims
    qk_w = kv_lora + LANES
    tm = _row_tile(T, align=LANES)
    wa, wkvT, wkT, qn, kvn, kvnc, wnope, wpe, wpesw, wuk = wts
    cosq, sinq, cosk, sink = tabs
    full = lambda arr: pl.BlockSpec(arr.shape, lambda b, i: (0,) * arr.ndim)
    row = lambda w: pl.BlockSpec((1, tm, w), lambda b, i: (b, i, 0))
    col = lambda r: pl.BlockSpec((1, r, tm), lambda b, i: (b, 0, i))
    kern = functools.partial(_mla_proj_kernel, q_lora=q_lora, kv_lora=kv_lora, rope=rope,
                             heads=heads, nope=nope, scale=scale)
    return pl.pallas_call(
        kern,
        grid=(B, T // tm),
        in_specs=[row(D), full(g), full(wa), full(wkvT), full(wkT), full(qn), full(kvn), full(kvnc), full(wnope),
                  full(wpe), full(wpesw), full(wuk),
                  pl.BlockSpec((tm, LANES), lambda b, i: (i, 0)),
                  pl.BlockSpec((tm, LANES), lambda b, i: (i, 0)),
                  pl.BlockSpec((rope, tm), lambda b, i: (0, i)),
                  pl.BlockSpec((rope, tm), lambda b, i: (0, i))],
        out_specs=[row(heads * qk_w), row(kv_lora), col(rope), row(kv_lora), col(qk_w)],
        out_shape=[jax.ShapeDtypeStruct((B, T, heads * qk_w), BF16),
                   jax.ShapeDtypeStruct((B, T, kv_lora), F32),
                   jax.ShapeDtypeStruct((B, rope, T), F32),
                   jax.ShapeDtypeStruct((B, T, kv_lora), BF16),
                   jax.ShapeDtypeStruct((B, qk_w, T), BF16)],
        compiler_params=_cparams("parallel", "parallel"),
        name="mla_project",
    )(x, g, wa, wkvT, wkT, qn, kvn, kvnc, wnope, wpe, wpesw, wuk, cosq, sinq, cosk, sink)


def _softmax_update(s, vals, m_scr, l_scr, acc_scr, rows=slice(None)):
    m_prev = m_scr[rows, :]
    m_next = jnp.maximum(m_prev, jnp.max(s, axis=1, keepdims=True))
    p = jnp.exp2(s - jnp.tile(m_next, (1, s.shape[1] // LANES)))
    alpha = jnp.exp2(m_prev - m_next)
    l_scr[rows, :] = alpha * l_scr[rows, :] + jnp.sum(p, axis=1, keepdims=True)
    acc_scr[rows, :] = (acc_scr[rows, :] * jnp.tile(alpha, (1, acc_scr.shape[1] // LANES))
                        + _dot(p.astype(BF16), vals))
    m_scr[rows, :] = m_next


def _attn_prompt_kernel(q_ref, c_ref, kT_ref, o_ref, q_scr, m_scr, l_scr, acc_scr,
                        *, heads, tq, small, levels, kv_lora, sub_rows):
    qi = pl.program_id(1)
    rows = tq * heads
    qk_w = kv_lora + LANES
    for h in range(heads):
        q_scr[h * tq:(h + 1) * tq, :] = q_ref[0, :, h * qk_w:(h + 1) * qk_w]
    m_scr[...] = jnp.full(m_scr.shape, -jnp.inf, F32)
    l_scr[...] = jnp.zeros(l_scr.shape, F32)
    acc_scr[...] = jnp.zeros(acc_scr.shape, F32)
    n_sub = rows // sub_rows

    def chunk(j, n_small, masked):
        tk = n_small * small
        start = pl.multiple_of(j * small, small)
        kc = c_ref[0, pl.ds(start, tk), :]
        kT = jnp.concatenate([kT_ref[0, j + i] for i in range(n_small)], axis=1)
        score = lambda i: _dot(q_scr[i * sub_rows:(i + 1) * sub_rows, :], kT)
        s_next = score(0)
        for i in range(n_sub):
            s = s_next
            if i + 1 < n_sub:
                s_next = score(i + 1)
            if masked:
                row = i * sub_rows + lax.broadcasted_iota(jnp.int32, (sub_rows, tk), 0)
                key = start + lax.broadcasted_iota(jnp.int32, (sub_rows, tk), 1)
                s = jnp.where(key <= qi * tq + (row & (tq - 1)), s, NEG_INF)
            _softmax_update(s, kc, m_scr, l_scr, acc_scr, slice(i * sub_rows, (i + 1) * sub_rows))

    n_full = (qi * tq + 1) // small
    n_total = (qi * tq + tq - 1) // small + 1
    done = 0
    for size in levels:
        count = (n_full - done) // size

        def full_body(j, carry, size=size, done=done):
            chunk(done + j * size, size, False)
            return carry

        lax.fori_loop(0, count, full_body, 0)
        done = done + count * size
    n_rest = n_total - done
    for size in range(1, levels[-1] + 1):
        def rest_body(j, carry, size=size):
            chunk(done, size, True)
            return carry

        lax.fori_loop(0, (n_rest == size).astype(jnp.int32), rest_body, 0)
    inv = 1.0 / l_scr[...]
    o = acc_scr[...] * jnp.tile(inv, (1, kv_lora // LANES))
    for h in range(heads):
        o_ref[0, :, h * kv_lora:(h + 1) * kv_lora] = o[h * tq:(h + 1) * tq].astype(o_ref.dtype)


def _attn_prompt(q, c_kv, k_t, heads):
    B, T, C = c_kv.shape
    qk_w = C + LANES
    tq, small, levels = ATTN_Q_TOKENS, ATTN_K_SMALL, ATTN_K_LEVELS
    assert T % small == 0 and small % tq == 0 and tq & (tq - 1) == 0 and C % LANES == 0
    assert all(a % b == 0 for a, b in zip(levels, levels[1:]))
    rows = tq * heads
    kern = functools.partial(_attn_prompt_kernel, heads=heads, tq=tq, small=small, levels=levels, kv_lora=C,
                             sub_rows=math.gcd(rows, ATTN_SUB_ROWS))
    return pl.pallas_call(
        kern,
        grid=(B, T // tq),
        in_specs=[pl.BlockSpec((1, tq, heads * qk_w), lambda b, i: (b, i, 0)),
                  pl.BlockSpec((1, T, C), lambda b, i: (b, 0, 0)),
                  pl.BlockSpec((1, T // small, qk_w, small), lambda b, i: (b, 0, 0, 0))],
        out_specs=pl.BlockSpec((1, tq, heads * C), lambda b, i: (b, i, 0)),
        out_shape=jax.ShapeDtypeStruct((B, T, heads * C), BF16),
        scratch_shapes=[pltpu.VMEM((rows, qk_w), BF16),
                        pltpu.VMEM((rows, LANES), F32), pltpu.VMEM((rows, LANES), F32),
                        pltpu.VMEM((rows, C), F32)],
        compiler_params=_cparams("parallel", "arbitrary"),
        name="attn_prompt",
    )(q, c_kv, k_t)


def _attn_sample_kernel(pt_ref, ql_ref, qp_ref, cn_ref, knT_ref, cache_c, cache_kT, o_ref,
                        cbuf, kbuf, sem, m_scr, l_scr, acc_scr,
                        *, layer, group, n_groups, total, slots, parts, page, heads):
    b = pl.program_id(0)

    def copies(gid):
        slot = gid % slots
        out = []
        for g in range(group):
            pg = pt_ref[gid * group + g]
            out.append(pltpu.make_async_copy(cache_c.at[layer, pg], cbuf.at[slot, pl.ds(g * page, page), :],
                                             sem.at[0, slot]))
            out.append(pltpu.make_async_copy(cache_kT.at[layer, pg], kbuf.at[slot, :, pl.ds(g * page, page)],
                                             sem.at[1, slot]))
        return out

    @pl.when(b == 0)
    def _():
        for g0 in range(min(slots - 1, total)):
            for cp in copies(g0):
                cp.start()

    m_scr[...] = jnp.full(m_scr.shape, -jnp.inf, F32)
    l_scr[...] = jnp.zeros(l_scr.shape, F32)
    acc_scr[...] = jnp.zeros(acc_scr.shape, F32)
    ql = ql_ref[0]
    qp = qp_ref[0]

    def body(j, carry):
        gid = b * n_groups + j

        @pl.when(gid + slots - 1 < total)
        def _():
            for cp in copies(gid + slots - 1):
                cp.start()

        for cp in copies(gid):
            cp.wait()
        slot = gid % slots
        part = group * page // parts
        kcs = [cbuf[slot, i * part:(i + 1) * part, :].astype(BF16) for i in range(parts)]
        kTs = [kbuf[slot, :, i * part:(i + 1) * part].astype(BF16) for i in range(parts)]
        scores = [_dot_nt(ql, kcs[i]) + _dot(qp, kTs[i]) for i in range(parts)]
        for i in range(parts):
            _softmax_update(scores[i], kcs[i], m_scr, l_scr, acc_scr)
        return carry

    lax.fori_loop(0, n_groups, body, 0)

    cn = cn_ref[0]
    s = _dot_nt(ql, cn) + _dot(qp, knT_ref[0])
    rows = s.shape[0]
    tok = lax.broadcasted_iota(jnp.int32, (rows, LANES), 0) // heads
    key = lax.broadcasted_iota(jnp.int32, (rows, LANES), 1)
    _softmax_update(jnp.where(key <= tok, s, NEG_INF), cn, m_scr, l_scr, acc_scr)
    inv = 1.0 / l_scr[...]
    o_ref[0] = (acc_scr[...] * jnp.tile(inv, (1, acc_scr.shape[1] // LANES))).astype(o_ref.dtype)


def _attn_sample(q_lat, q_pe, c_new, k_new_t, cache_c, cache_kT, page_table, layer, heads):
    DB, rows, C = q_lat.shape
    R = cache_kT.shape[2]
    page = cache_c.shape[2]
    n_pages = page_table.shape[1]
    assert page == LANES and rows // heads <= LANES
    group = math.gcd(PAGES_PER_GROUP, n_pages)
    n_groups = n_pages // group
    pt = page_table.reshape(-1).astype(jnp.int32)
    per_b = lambda shape: pl.BlockSpec((1,) + shape, lambda b, pt_ref: (b, 0, 0))
    slots = SAMPLE_SLOTS
    kern = functools.partial(_attn_sample_kernel, layer=layer, group=group, n_groups=n_groups,
                             total=DB * n_groups, slots=slots, parts=math.gcd(SAMPLE_PARTS, group), page=page,
                             heads=heads)
    grid_spec = pltpu.PrefetchScalarGridSpec(
        num_scalar_prefetch=1,
        grid=(DB,),
        in_specs=[per_b((rows, C)), per_b((rows, R)), per_b((LANES, C)), per_b((R, LANES)),
                  pl.BlockSpec(memory_space=pl.ANY), pl.BlockSpec(memory_space=pl.ANY)],
        out_specs=per_b((rows, C)),
        scratch_shapes=[pltpu.VMEM((slots, group * page, C), F32), pltpu.VMEM((slots, R, group * page), F32),
                        pltpu.SemaphoreType.DMA((2, slots)),
                        pltpu.VMEM((rows, LANES), F32), pltpu.VMEM((rows, LANES), F32),
                        pltpu.VMEM((rows, C), F32)],
    )
    return pl.pallas_call(
        kern,
        grid_spec=grid_spec,
        out_shape=jax.ShapeDtypeStruct((DB, rows, C), BF16),
        compiler_params=_cparams("arbitrary"),
        name="attn_sample",
    )(pt, q_lat, q_pe, c_new, k_new_t, cache_c, cache_kT)


def _mix_ffn_kernel(x_ref, p_ref, q_ref, wo_ref, g_ref, wup_ref, wdn_ref, gnext_ref, *rest, mla_heads, emit_h):
    outs, (xn_scr, acc_scr, h_scr, *cat_scr) = rest[:1 + emit_h], rest[1 + emit_h:]
    nnext_ref = outs[-1]
    k = pl.program_id(2)

    @pl.when(k == 0)
    def _():
        if mla_heads:
            kv_lora, vhead = q_ref.shape[1:]
            for h in range(mla_heads):
                oh = _dot(p_ref[:, h * kv_lora:(h + 1) * kv_lora], q_ref[h])
                cat_scr[0][:, h * vhead:(h + 1) * vhead] = oh.astype(BF16)
            mixed = cat_scr[0][...]
        else:
            mixed = (p_ref[...] * q_ref[...]).astype(BF16)
        h = x_ref[...] + _dot(mixed, wo_ref[...])
        h_scr[...] = h
        xn_scr[...] = _rms(h, g_ref[...]).astype(BF16)
        acc_scr[...] = jnp.zeros(acc_scr.shape, F32)

    u = jnp.maximum(_dot(xn_scr[...], wup_ref[...]), 0.0)
    acc_scr[...] += _dot((u * u).astype(BF16), wdn_ref[...])

    @pl.when(k == pl.num_programs(2) - 1)
    def _():
        out = h_scr[...] + acc_scr[...]
        if emit_h:
            outs[0][...] = out
        nnext_ref[...] = _rms(out, gnext_ref[...])


def _mix_ffn(x, p, q, wo, g, wup, wdn, g_next, *, mla_heads=0, emit_h=True, window=None):
    N, D = x.shape
    FF = wup.shape[1]
    stride, offset, length = (N, 0, N) if window is None else window
    n_seq = N // stride
    tf = min(FF_TILE, FF)
    if offset == 0 and length == stride:
        tm = _row_tile(length)
        rows = lambda w: pl.BlockSpec((tm, w), lambda b, i, k: (b * (stride // tm) + i, 0))
    else:
        tm = _row_tile(length, FFN_WINDOW_TILE)
        assert stride % 8 == 0 and offset % 8 == 0 and tm % 8 == 0
        rows = lambda w: pl.BlockSpec((pl.Element(tm), pl.Element(w)),
                                      lambda b, i, k: (pl.multiple_of(b * stride + offset + i * tm, 8), 0))
    const = lambda arr: pl.BlockSpec(arr.shape, lambda b, i, k: (0,) * arr.ndim)
    out_spec = pl.BlockSpec((tm, D), lambda b, i, k: (b * (length // tm) + i, 0))
    out_sds = jax.ShapeDtypeStruct((n_seq * length, D), F32)
    n_out = 2 if emit_h else 1
    scratch = [pltpu.VMEM((tm, D), BF16), pltpu.VMEM((tm, D), F32), pltpu.VMEM((tm, D), F32)]
    if mla_heads:
        scratch.append(pltpu.VMEM((tm, wo.shape[0]), BF16))
    res = pl.pallas_call(
        functools.partial(_mix_ffn_kernel, mla_heads=mla_heads, emit_h=emit_h),
        grid=(n_seq, length // tm, FF // tf),
        in_specs=[rows(D), rows(p.shape[1]), const(q) if mla_heads else rows(D), const(wo), const(g),
                  pl.BlockSpec((D, tf), lambda b, i, k: (0, k)),
                  pl.BlockSpec((tf, D), lambda b, i, k: (k, 0)),
                  const(g_next)],
        out_specs=[out_spec] * n_out,
        out_shape=[out_sds] * n_out,
        scratch_shapes=scratch,
        compiler_params=_cparams("parallel", "parallel", "arbitrary"),
        name="mix_ffn",
    )(x, p, q, wo, g, wup, wdn, g_next)
    return res if emit_h else (None, res[0])


def _rwkv_proj_kernel(n_ref, xp_ref, mu_ref, wr_ref, wk_ref, wv_ref, dw0_ref, dw1_ref, dw2_ref,
                      aw0_ref, aw1_ref, aw2_ref, gw1_ref, gw2_ref,
                      r_ref, k_ref, v_ref, d_ref, a_ref, g_ref, *, seq_len):
    n = n_ref[...]
    if seq_len is None:
        x_prev = xp_ref[...]
    else:
        tm = n.shape[0]
        inside = ((pl.program_id(0) * tm) % seq_len != 0).astype(F32)
        row = lax.broadcasted_iota(jnp.int32, (tm, 1), 0)
        x_prev = jnp.where(row == 0, xp_ref[7:8, :] * inside, pltpu.roll(n, 1, axis=0))
    xx = x_prev - n
    mix = lambda j: (n + xx * mu_ref[j:j + 1, :]).astype(BF16)
    r_ref[...] = _dot(mix(0), wr_ref[...])
    k_ref[...] = _dot(mix(2), wk_ref[...])
    v_ref[...] = _dot(mix(3), wv_ref[...])
    z = dw0_ref[...] + _dot(jnp.tanh(_dot(mix(1), dw1_ref[...])).astype(BF16), dw2_ref[...])
    d_ref[...] = (-math.exp(-0.5)) / (1.0 + jnp.exp(-z))
    za = aw0_ref[...] + _dot(_dot(mix(4), aw1_ref[...]).astype(BF16), aw2_ref[...])
    a_ref[...] = 1.0 / (1.0 + jnp.exp(-za))
    zg = _dot(mix(5), gw1_ref[...])
    g_ref[...] = _dot((1.0 / (1.0 + jnp.exp(-zg))).astype(BF16), gw2_ref[...])


def _rwkv_project(n, x_prev, wts, seq_len=None):
    N, D = n.shape
    tm = _row_tile(N if seq_len is None else seq_len, RWKV_ROW_TILE)
    row = pl.BlockSpec((tm, D), lambda i: (i, 0))
    full = lambda arr: pl.BlockSpec(arr.shape, lambda i: (0,) * arr.ndim)
    if seq_len is None:
        prev_spec = row
    else:
        x_prev = n
        prev_spec = pl.BlockSpec((8, D), lambda i: (jnp.maximum(i * (tm // 8) - 1, 0), 0))
    return pl.pallas_call(
        functools.partial(_rwkv_proj_kernel, seq_len=seq_len),
        grid=(N // tm,),
        in_specs=[row, prev_spec] + [full(w) for w in wts],
        out_specs=[row] * 6,
        out_shape=[jax.ShapeDtypeStruct((N, D), F32)] * 6,
        compiler_params=_cparams("parallel"),
        name="rwkv_project",
    )(n, x_prev, *wts)


def _wkv_kernel(r_ref, k_ref, v_ref, d_ref, a_ref, kk_ref, ka_ref, rk_ref, lnw_ref, lnb_ref, s0_ref,
                y_ref, sout_ref, st_scr, *, L, t_valid, nb, pairs, hd):
    c = pl.program_id(0)
    L2 = 2 * L
    W = 2 * hd
    P = range(nb * pairs)
    bi = [q // pairs for q in P]
    hp = [q % pairs for q in P]
    m0 = lax.broadcasted_iota(jnp.int32, (1, W), 1) < hd

    @pl.when(c == 0)
    def _():
        z = jnp.zeros((hd, hd), F32)
        for p in P:
            s_bd = jnp.concatenate([jnp.concatenate([s0_ref[bi[p], 2 * hp[p]], z], axis=1),
                                    jnp.concatenate([z, s0_ref[bi[p], 2 * hp[p] + 1]], axis=1)], axis=0)
            st_scr[p] = s_bd.T

    valid = (c * L + lax.broadcasted_iota(jnp.int32, (L, 1), 0)) < t_valid
    tri = (lax.broadcasted_iota(jnp.int32, (L, L), 0) >= lax.broadcasted_iota(jnp.int32, (L, L), 1))
    tri3 = jnp.tile(tri.astype(BF16), (1, 3))
    ones3 = jnp.ones((3 * L, W), BF16)
    i2 = lax.broadcasted_iota(jnp.int32, (L2, L2), 0)
    j2 = lax.broadcasted_iota(jnp.int32, (L2, L2), 1)
    mask_s = j2 < i2
    mask_i = j2 <= i2
    eye = (i2 == j2).astype(F32)

    def stack(x):
        return jnp.concatenate([jnp.where(m0, x, 0.0), jnp.where(m0, 0.0, x)], axis=0).astype(BF16)

    def head_sum(x):
        s_a = jnp.sum(jnp.where(m0, x, 0.0), axis=1, keepdims=True)
        s_b = jnp.sum(jnp.where(m0, 0.0, x), axis=1, keepdims=True)
        return jnp.where(m0, s_a, s_b)

    def split3(x):
        hi = x.astype(BF16)
        r1 = x - hi.astype(F32)
        mid = r1.astype(BF16)
        lo = (r1 - mid.astype(F32)).astype(BF16)
        return jnp.concatenate([hi, mid, lo], axis=0)

    sl = [slice(hp[p] * W, (hp[p] + 1) * W) for p in P]
    load = lambda ref, p: jnp.where(valid, ref[bi[p], :, sl[p]], 0.0)
    r = [load(r_ref, p) for p in P]
    k = [load(k_ref, p) for p in P]
    v = [load(v_ref, p) for p in P]
    d = [load(d_ref, p) for p in P]
    a = [load(a_ref, p) for p in P]
    d3 = [split3(d[p]) for p in P]
    cum = [_dot(tri3, d3[p]) for p in P]
    kk = [k[p] * kk_ref[:, sl[p]] for p in P]
    kk = [kk[p] / jnp.maximum(jnp.sqrt(head_sum(kk[p] * kk[p])), 1e-12) for p in P]
    kp = [k[p] * (1.0 + (a[p] - 1.0) * ka_ref[:, sl[p]]) for p in P]
    bv = [kk[p] * a[p] for p in P]
    e_neg = [jnp.exp(-cum[p]) for p in P]
    a_st = [stack(-kk[p] * jnp.exp(cum[p] - d[p])) for p in P]
    r_st = [stack(r[p] * jnp.exp(cum[p])) for p in P]
    b_st = [stack(bv[p] * e_neg[p]) for p in P]
    k_st = [stack(kp[p] * e_neg[p]) for p in P]
    v_st = [stack(v[p]) for p in P]
    pm = [_dot_nt(jnp.concatenate([a_st[p], r_st[p]], axis=0), jnp.concatenate([b_st[p], k_st[p]], axis=0))
          for p in P]
    m_ab = [jnp.where(mask_s, pm[p][:L2, :L2], 0.0) for p in P]
    m_ak = [jnp.where(mask_s, pm[p][:L2, L2:], 0.0).astype(BF16) for p in P]
    m_rb = [jnp.where(mask_i, pm[p][L2:, :L2], 0.0).astype(BF16) for p in P]
    m_rk = [jnp.where(mask_i, pm[p][L2:, L2:], 0.0).astype(BF16) for p in P]
    st = [st_scr[p] for p in P]
    stb = [st[p].astype(BF16) for p in P]
    rhs = [_dot(jnp.concatenate([a_st[p], m_ak[p]], axis=1), jnp.concatenate([stb[p], v_st[p]], axis=0)) for p in P]
    inv = [eye + m_ab[p] for p in P]
    pw = [m_ab[p].astype(BF16) for p in P]
    pw = [_dot(pw[p], pw[p]).astype(BF16) for p in P]
    levels = int(math.log2(L))
    for lvl in range(1, levels):
        last = lvl == levels - 1
        z = [_dot(pw[p], inv[p].astype(BF16) if last else jnp.concatenate([inv[p].astype(BF16), pw[p]], axis=1))
             for p in P]
        inv = [inv[p] + z[p][:, :L2] for p in P]
        if not last:
            pw = [z[p][:, L2:].astype(BF16) for p in P]
    u_st = [_dot(inv[p].astype(BF16), rhs[p].astype(BF16)).astype(BF16) for p in P]
    y_st = [_dot(jnp.concatenate([r_st[p], m_rb[p], m_rk[p]], axis=1),
                 jnp.concatenate([stb[p], u_st[p], v_st[p]], axis=0)) for p in P]
    decay_col = [jnp.exp(_dot_tn(d3[p], ones3)) for p in P]
    e_rest = [jnp.exp(cum[p][L - 1:L, :] - cum[p]) for p in P]
    for p in P:
        st_scr[p] = decay_col[p] * st[p] + _dot_tn(
            jnp.concatenate([stack(bv[p] * e_rest[p]), stack(kp[p] * e_rest[p])], axis=0),
            jnp.concatenate([u_st[p], v_st[p]], axis=0))
    for p in P:
        y = y_st[p][:L] + y_st[p][L:]
        mean = head_sum(y) * (1.0 / hd)
        yc = y - mean
        var = head_sum(yc * yc) * (1.0 / hd)
        yn = yc * lax.rsqrt(var + GN_EPS) * lnw_ref[:, sl[p]] + lnb_ref[:, sl[p]]
        y_ref[bi[p], :, sl[p]] = yn + head_sum(r[p] * kp[p] * rk_ref[:, sl[p]]) * v[p]

    @pl.when(c == pl.num_programs(0) - 1)
    def _():
        for p in P:
            s_bd = st_scr[p].T
            sout_ref[bi[p], 2 * hp[p]] = s_bd[:hd, :hd]
            sout_ref[bi[p], 2 * hp[p] + 1] = s_bd[hd:, hd:]


def _wkv(r, k, v, d, a, params, s0, L, t_valid):
    B, T, D = r.shape
    H, hd = s0.shape[1], s0.shape[2]
    assert 2 * hd == LANES and H % 2 == 0 and T % L == 0 and (2 * L) % LANES == 0
    pairs = H // 2
    seq = pl.BlockSpec((B, L, D), lambda c: (0, c, 0))
    vec = pl.BlockSpec((1, D), lambda c: (0, 0))
    state = pl.BlockSpec((B, H, hd, hd), lambda c: (0, 0, 0, 0))
    kern = functools.partial(_wkv_kernel, L=L, t_valid=t_valid, nb=B, pairs=pairs, hd=hd)
    return pl.pallas_call(
        kern,
        grid=(T // L,),
        in_specs=[seq] * 5 + [vec] * 5 + [state],
        out_specs=[seq, state],
        out_shape=[jax.ShapeDtypeStruct((B, T, D), F32), jax.ShapeDtypeStruct(s0.shape, F32)],
        scratch_shapes=[pltpu.VMEM((B * pairs, LANES, LANES), F32)],
        compiler_params=_cparams("arbitrary"),
        name="wkv",
    )(r, k, v, d, a, *params, s0)


def _wkv_sample_kernel(r_ref, k_ref, v_ref, d_ref, a_ref, kk_ref, ka_ref, rk_ref, lnw_ref, lnb_ref, s0_ref,
                       y_ref, sout_ref, w_scr, a_scr, b_scr, kp_scr, yraw_scr, *, steps, hd):
    for t in range(steps):
        kt = k_ref[t]
        at = a_ref[t]
        kk = kt * kk_ref[...]
        kk = kk / jnp.maximum(jnp.sqrt(jnp.sum(kk * kk, axis=0, keepdims=True)), 1e-12)
        w_scr[t] = jnp.exp(d_ref[t])
        a_scr[t] = -kk
        b_scr[t] = kk * at
        kp_scr[t] = kt * (1.0 + (at - 1.0) * ka_ref[...])

    def body(i, carry):
        rows = [i * WKV_SAMPLE_ROWS + j for j in range(WKV_SAMPLE_ROWS)]
        s = [s0_ref[vi] for vi in rows]
        for t in range(steps):
            sa = [jnp.sum(s[j] * a_scr[t], axis=0, keepdims=True) for j in range(len(rows))]
            s = [s[j] * w_scr[t] + sa[j] * b_scr[t] + v_ref[t, pl.ds(vi, 1), :] * kp_scr[t]
                 for j, vi in enumerate(rows)]
            for j, vi in enumerate(rows):
                yraw_scr[t, pl.ds(vi, 1), :] = jnp.sum(s[j] * r_ref[t], axis=0, keepdims=True)
        for j, vi in enumerate(rows):
            sout_ref[vi] = s[j]
        return carry

    assert hd % WKV_SAMPLE_ROWS == 0
    lax.fori_loop(0, hd // WKV_SAMPLE_ROWS, body, 0)
    for t in range(steps):
        y = yraw_scr[t]
        mean = jnp.mean(y, axis=0, keepdims=True)
        yc = y - mean
        var = jnp.mean(yc * yc, axis=0, keepdims=True)
        yn = yc * lax.rsqrt(var + GN_EPS) * lnw_ref[...] + lnb_ref[...]
        bonus = jnp.sum(r_ref[t] * kp_scr[t] * rk_ref[...], axis=0, keepdims=True)
        y_ref[t] = yn + bonus * v_ref[t]


def _wkv_sample(r, k, v, d, a, params, s0):
    S, D, DB = r.shape
    H, hd = s0.shape[0], s0.shape[1]
    seq = pl.BlockSpec((S, hd, DB), lambda h: (0, h, 0))
    vec = pl.BlockSpec((hd, DB), lambda h: (h, 0))
    state = pl.BlockSpec((None, hd, hd, DB), lambda h: (h, 0, 0, 0))
    kern = functools.partial(_wkv_sample_kernel, steps=S, hd=hd)
    return pl.pallas_call(
        kern,
        grid=(H,),
        in_specs=[seq] * 5 + [vec] * 5 + [state],
        out_specs=[seq, state],
        out_shape=[jax.ShapeDtypeStruct((S, D, DB), F32), jax.ShapeDtypeStruct(s0.shape, F32)],
        scratch_shapes=[pltpu.VMEM((S, hd, DB), F32)] * 5,
        compiler_params=_cparams("parallel"),
        name="wkv_sample",
    )(r, k, v, d, a, *params, s0)


def _rope_tables(pos, rope, heads):
    half = rope // 2
    inv_freq = ROPE_THETA ** (-jnp.arange(half, dtype=F32) / half)
    ang = pos.astype(F32)[:, None] * inv_freq[None, :]
    cos, sin = jnp.cos(ang), jnp.sin(ang)
    cos2 = jnp.concatenate([cos, cos], axis=-1)
    sin2 = jnp.concatenate([-sin, sin], axis=-1)
    pad = lambda t: jnp.pad(t, ((0, 0), (0, LANES - rope)))
    return pad(cos2), pad(sin2), cos2.T, sin2.T


def _swap_halves(w, width):
    lead = w.shape[:-1]
    g = w.reshape(lead + (-1, 2, width // 2))
    return g[..., ::-1, :].reshape(w.shape)


def _pad_heads(w, heads, rope):
    K = w.shape[0]
    return jnp.pad(w.reshape(K, heads, rope), ((0, 0), (0, 0), (0, LANES - rope))).reshape(K, heads * LANES)


def kernel(x_prompt, x_sample, cache_kv_latent, cache_k_rope, state_wkv, state_shift, page_table, meta_tokens, norm_mix, norm_ffn, norm_final, mla_w_qkv_a, mla_q_a_norm, mla_kv_a_norm, mla_w_q_b, mla_w_kv_b, mla_w_o, rwkv_mu, rwkv_w_r, rwkv_w_k, rwkv_w_v, rwkv_w_o, rwkv_decay_w0, rwkv_decay_w1, rwkv_decay_w2, rwkv_a_w0, rwkv_a_w1, rwkv_a_w2, rwkv_g_w1, rwkv_g_w2, rwkv_k_k, rwkv_k_a, rwkv_r_k, rwkv_ln_w, rwkv_ln_b, ffn_w_up, ffn_w_down):
    B, SEQ, D = x_prompt.shape
    DB, S, _ = x_sample.shape
    n_meta = meta_tokens.shape[0]
    T = n_meta + SEQ
    Tp = -(-T // SEQ_ALIGN) * SEQ_ALIGN
    page = cache_kv_latent.shape[2]
    past_len = page_table.shape[1] * page
    kv_lora = cache_kv_latent.shape[-1]
    rope = cache_k_rope.shape[-1]
    q_lora = mla_q_a_norm.shape[-1]
    H, hd = state_wkv.shape[2], state_wkv.shape[3]
    qk = mla_w_q_b.shape[-1]
    kvb = mla_w_kv_b.shape[-1]
    ov = mla_w_o.shape[1]
    heads = (qk + ov - kvb) // rope
    nope = qk // heads - rope
    vhead = ov // heads
    qk_w = kv_lora + LANES
    scale = float(nope + rope) ** -0.5 * LOG2E
    dims = (q_lora, kv_lora, rope, heads, nope, scale)

    row = lambda vec: vec.reshape(1, -1).astype(F32)
    b16 = lambda w: w.astype(BF16)

    meta = jnp.broadcast_to(meta_tokens.astype(F32)[None], (B, n_meta, D))
    hp = jnp.concatenate([meta, x_prompt, jnp.zeros((B, Tp - T, D), F32)], axis=1)
    hs = x_sample.reshape(1, DB * S, D)

    l = 0
    wqkv = mla_w_qkv_a[l]
    wa = b16(wqkv[:, :q_lora + kv_lora])
    wkvT = b16(wqkv[:, q_lora:q_lora + kv_lora].T)
    wk = wqkv[:, q_lora + kv_lora:]
    wkT = b16(jnp.concatenate([wk, _swap_halves(wk, rope)], axis=1).T)
    wqb = mla_w_q_b[l].reshape(q_lora, heads, nope + rope)
    wnope = b16(wqb[..., :nope].reshape(q_lora, heads * nope))
    wpe = wqb[..., nope:].reshape(q_lora, heads * rope)
    wpesw = b16(_pad_heads(_swap_halves(wpe, rope), heads, rope))
    wpe = b16(_pad_heads(wpe, heads, rope))
    wkvb = mla_w_kv_b[l].reshape(kv_lora, heads, nope + vhead)
    wuk = b16(jnp.transpose(wkvb[..., :nope], (1, 2, 0)))
    wuv = b16(jnp.transpose(wkvb[..., nope:], (1, 0, 2)))
    wo = b16(mla_w_o[l])
    kvn = mla_kv_a_norm[l].astype(F32)
    mla_w = (wa, wkvT, wkT, row(mla_q_a_norm[l]), row(kvn), kvn.reshape(-1, 1), wnope, wpe, wpesw, wuk)

    tabs_p = _rope_tables(jnp.arange(Tp), rope, heads)
    tabs_s = _rope_tables(jnp.tile(past_len + jnp.arange(S), DB), rope, heads)

    q_p, c_p, kpeT_p, cb_p, kT_p = _mla_project(hp, row(norm_mix[0]), mla_w, tabs_p, dims)
    q_s, c_s, kpeT_s, cb_s, kT_s = _mla_project(hs, row(norm_mix[0]), mla_w, tabs_s, dims)

    ks = ATTN_K_SMALL
    kT_chunks = kT_p.reshape(B, qk_w, Tp // ks, ks).transpose(0, 2, 1, 3)
    o_p = _attn_prompt(q_p, cb_p, kT_chunks, heads)

    q_s = q_s.reshape(DB, S * heads, qk_w)
    c_new = jnp.pad(cb_s.reshape(DB, S, kv_lora), ((0, 0), (0, LANES - S), (0, 0)))
    k_new_t = kT_s[0, kv_lora:kv_lora + rope].reshape(rope, DB, S).transpose(1, 0, 2)
    k_new_t = jnp.pad(k_new_t, ((0, 0), (0, 0), (0, LANES - S)))
    o_s = _attn_sample(q_s[..., :kv_lora], q_s[..., kv_lora:kv_lora + rope], c_new, k_new_t,
                       cache_kv_latent, jnp.swapaxes(cache_k_rope, 2, 3), page_table, l, heads)

    wup0, wdn0 = b16(ffn_w_up[0]), b16(ffn_w_down[0])
    hp, np_ = _mix_ffn(hp.reshape(B * Tp, D), o_p.reshape(B * Tp, heads * kv_lora), wuv, wo,
                       row(norm_ffn[0]), wup0, wdn0, row(norm_mix[1]), mla_heads=heads)
    hs, ns_ = _mix_ffn(hs.reshape(DB * S, D), o_s.reshape(DB * S, heads * kv_lora), wuv, wo,
                       row(norm_ffn[0]), wup0, wdn0, row(norm_mix[1]), mla_heads=heads)

    np3 = np_.reshape(B, Tp, D)
    ns3 = ns_.reshape(DB, S, D)
    xprev_s = jnp.concatenate([state_shift[l].astype(F32)[:, None], ns3[:, :-1]], axis=1).reshape(DB * S, D)
    rw = (rwkv_mu[l].astype(F32), b16(rwkv_w_r[l]), b16(rwkv_w_k[l]), b16(rwkv_w_v[l]),
          row(rwkv_decay_w0[l]), b16(rwkv_decay_w1[l]), b16(rwkv_decay_w2[l]),
          row(rwkv_a_w0[l]), b16(rwkv_a_w1[l]), b16(rwkv_a_w2[l]),
          b16(rwkv_g_w1[l]), b16(rwkv_g_w2[l]))
    par = (rwkv_k_k[l], rwkv_k_a[l], rwkv_r_k[l], rwkv_ln_w[l], rwkv_ln_b[l])

    r_p, k_p, v_p, d_p, a_p, g_p = _rwkv_project(np_, None, rw, seq_len=Tp)
    r_s, k_s, v_s, d_s, a_s, g_s = _rwkv_project(ns_, xprev_s, rw)

    seq_p = lambda t: t.reshape(B, Tp, D)
    y_p, st_p = _wkv(seq_p(r_p), seq_p(k_p), seq_p(v_p), seq_p(d_p), seq_p(a_p), tuple(row(x) for x in par),
                     jnp.zeros((B, H, hd, hd), F32), WKV_CHUNK, T)

    lanes_b = lambda t: t.reshape(DB, S, D).transpose(1, 2, 0)
    par_b = tuple(jnp.broadcast_to(x.reshape(D, 1).astype(F32), (D, DB)) for x in par)
    y_s, st_s = _wkv_sample(lanes_b(r_s), lanes_b(k_s), lanes_b(v_s), lanes_b(d_s), lanes_b(a_s), par_b,
                            jnp.transpose(state_wkv[l].astype(F32), (1, 2, 3, 0)))
    y_s = y_s.transpose(2, 0, 1).reshape(DB * S, D)
    st_s = jnp.transpose(st_s, (3, 0, 1, 2))

    wo_r = b16(rwkv_w_o[l])
    wup1, wdn1 = b16(ffn_w_up[1]), b16(ffn_w_down[1])
    _, yp = _mix_ffn(hp, y_p.reshape(B * Tp, D), g_p, wo_r, row(norm_ffn[1]), wup1, wdn1, row(norm_final),
                     emit_h=False, window=(Tp, n_meta, SEQ))
    _, ys = _mix_ffn(hs, y_s, g_s, wo_r, row(norm_ffn[1]), wup1, wdn1, row(norm_final), emit_h=False)

    y_prompt = yp.reshape(B, SEQ, D)
    y_sample = ys.reshape(DB, S, D)
    k_rope_p = jnp.swapaxes(kpeT_p[:, :, :T], 1, 2)
    k_rope_s = kpeT_s[0].reshape(rope, DB, S).transpose(1, 2, 0)
    return (y_prompt, y_sample,
            c_p[None, :, :T], k_rope_p[None],
            c_s.reshape(1, DB, S, kv_lora), k_rope_s[None],
            st_p[None], np3[None, :, T - 1], st_s[None], ns3[None, :, S - 1])
```

```python
import functools
import math

import jax
import jax.numpy as jnp
from jax import lax
from jax.experimental import pallas as pl
from jax.experimental.pallas import tpu as pltpu

F32 = jnp.float32
BF16 = jnp.bfloat16

RMS_EPS = 1e-6
GN_EPS = 64e-5
KK_NORM_EPS = 1e-12
NEG_INF = -1e30
ROPE_THETA = 10000.0
LOG2E = 1.4426950408889634

LANES = 128
VMEM_LIMIT_BYTES = 56 * 1024 * 1024

ATTN_Q_TOKENS = 256
ATTN_K_SMALL = 256
ATTN_K_LEVELS = (4, 2)
ATTN_SUB_ROWS = 256
SEQ_ALIGN = 256
ROW_TILE = 768
RWKV_ROW_TILE = 384
FF_TILE = 1024
FFN_WINDOW_TILE = 1024
WKV_CHUNK = 64
WKV_SAMPLE_ROWS = 2
PAGES_PER_GROUP = 16
SAMPLE_SLOTS = 3
SAMPLE_PARTS = 4


def _cparams(*sem):
    return pltpu.CompilerParams(dimension_semantics=sem, vmem_limit_bytes=VMEM_LIMIT_BYTES)


def _dot(a, b):
    return jnp.dot(a, b, preferred_element_type=F32)


def _dot_nt(a, b):
    return lax.dot_general(a, b, (((1,), (1,)), ((), ())), preferred_element_type=F32)


def _dot_tn(a, b):
    return lax.dot_general(a, b, (((0,), (0,)), ((), ())), preferred_element_type=F32)


def _rms(x, g):
    return x * lax.rsqrt(jnp.mean(x * x, axis=-1, keepdims=True) + RMS_EPS) * g


def _row_tile(n, target=ROW_TILE, align=8):
    best = None
    for t in range(align, min(n, target) + 1, align):
        if n % t == 0:
            best = t
    assert best is not None, n
    return best


def _mla_proj_kernel(x_ref, g_ref, wa_ref, wkvT_ref, wkT_ref, qn_ref, kvn_ref, kvnc_ref, wnope_ref, wpe_ref,
                     wpesw_ref, wuk_ref, cosq_ref, sinq_ref, cosk_ref, sink_ref,
                     q_ref, ckv_ref, kpeT_ref, ckvb_ref, kT_ref,
                     *, q_lora, kv_lora, rope, heads, nope, scale):
    tm = x_ref.shape[1]
    qk_w = kv_lora + LANES
    n = _rms(x_ref[0], g_ref[...]).astype(BF16)
    a = _dot(n, wa_ref[...])
    c_q = _rms(a[:, :q_lora], qn_ref[...]).astype(BF16)
    c_kv = _rms(a[:, q_lora:], kvn_ref[...])
    ckv_ref[0] = c_kv
    ckvb_ref[0] = c_kv.astype(BF16)
    a_kv_t = _dot_nt(wkvT_ref[...], n)
    c_kv_t = a_kv_t * lax.rsqrt(jnp.mean(a_kv_t * a_kv_t, axis=0, keepdims=True) + RMS_EPS) * kvnc_ref[...]
    a_t = _dot_nt(wkT_ref[...], n)
    k_pe_t = a_t[:rope] * cosk_ref[...] + a_t[rope:] * sink_ref[...]
    kpeT_ref[0] = k_pe_t
    kT_ref[0] = jnp.concatenate([c_kv_t.astype(BF16), k_pe_t.astype(BF16),
                                 jnp.zeros((LANES - rope, tm), BF16)], axis=0)
    cosq = jnp.tile(cosq_ref[...], (1, heads))
    sinq = jnp.tile(sinq_ref[...], (1, heads))
    q_pe = _dot(c_q, wpe_ref[...]) * cosq + _dot(c_q, wpesw_ref[...]) * sinq
    q_pe = (q_pe * scale).astype(BF16)
    q_nope = _dot(c_q, wnope_ref[...]).astype(BF16)
    for h in range(heads):
        q_lat = _dot(q_nope[:, h * nope:(h + 1) * nope], wuk_ref[h])
        q_ref[0, :, h * qk_w:h * qk_w + kv_lora] = (q_lat * scale).astype(BF16)
        q_ref[0, :, h * qk_w + kv_lora:(h + 1) * qk_w] = q_pe[:, h * LANES:(h + 1) * LANES]


def _mla_project(x, g, wts, tabs, dims):
    B, T, D = x.shape
    q_lora, kv_lora, rope, heads, nope, scale = dims
    qk_w = kv_lora + LANES
    tm = _row_tile(T, align=LANES)
    wa, wkvT, wkT, qn, kvn, kvnc, wnope, wpe, wpesw, wuk = wts
    cosq, sinq, cosk, sink = tabs
    full = lambda arr: pl.BlockSpec(arr.shape, lambda b, i: (0,) * arr.ndim)
    row = lambda w: pl.BlockSpec((1, tm, w), lambda b, i: (b, i, 0))
    col = lambda r: pl.BlockSpec((1, r, tm), lambda b, i: (b, 0, i))
    kern = functools.partial(_mla_proj_kernel, q_lora=q_lora, kv_lora=kv_lora, rope=rope,
                             heads=heads, nope=nope, scale=scale)
    return pl.pallas_call(
        kern,
        grid=(B, T // tm),
        in_specs=[row(D), full(g), full(wa), full(wkvT), full(wkT), full(qn), full(kvn), full(kvnc), full(wnope),
                  full(wpe), full(wpesw), full(wuk),
                  pl.BlockSpec((tm, LANES), lambda b, i: (i, 0)),
                  pl.BlockSpec((tm, LANES), lambda b, i: (i, 0)),
                  pl.BlockSpec((rope, tm), lambda b, i: (0, i)),
                  pl.BlockSpec((rope, tm), lambda b, i: (0, i))],
        out_specs=[row(heads * qk_w), row(kv_lora), col(rope), row(kv_lora), col(qk_w)],
        out_shape=[jax.ShapeDtypeStruct((B, T, heads * qk_w), BF16),
                   jax.ShapeDtypeStruct((B, T, kv_lora), F32),
                   jax.ShapeDtypeStruct((B, rope, T), F32),
                   jax.ShapeDtypeStruct((B, T, kv_lora), BF16),
                   jax.ShapeDtypeStruct((B, qk_w, T), BF16)],
        compiler_params=_cparams("parallel", "parallel"),
        name="mla_project",
    )(x, g, wa, wkvT, wkT, qn, kvn, kvnc, wnope, wpe, wpesw, wuk, cosq, sinq, cosk, sink)


def _softmax_update(s, vals, m_scr, l_scr, acc_scr, rows=slice(None)):
    m_prev = m_scr[rows, :]
    m_next = jnp.maximum(m_prev, jnp.max(s, axis=1, keepdims=True))
    p = jnp.exp2(s - jnp.tile(m_next, (1, s.shape[1] // LANES)))
    alpha = jnp.exp2(m_prev - m_next)
    l_scr[rows, :] = alpha * l_scr[rows, :] + jnp.sum(p, axis=1, keepdims=True)
    acc_scr[rows, :] = (acc_scr[rows, :] * jnp.tile(alpha, (1, acc_scr.shape[1] // LANES))
                        + _dot(p.astype(BF16), vals))
    m_scr[rows, :] = m_next


def _attn_prompt_kernel(q_ref, c_ref, kT_ref, o_ref, q_scr, m_scr, l_scr, acc_scr,
                        *, heads, tq, small, levels, kv_lora, sub_rows):
    qi = pl.program_id(1)
    rows = tq * heads
    qk_w = kv_lora + LANES
    for h in range(heads):
        q_scr[h * tq:(h + 1) * tq, :] = q_ref[0, :, h * qk_w:(h + 1) * qk_w]
    m_scr[...] = jnp.full(m_scr.shape, -jnp.inf, F32)
    l_scr[...] = jnp.zeros(l_scr.shape, F32)
    acc_scr[...] = jnp.zeros(acc_scr.shape, F32)
    n_sub = rows // sub_rows

    def chunk(j, n_small, masked):
        tk = n_small * small
        start = pl.multiple_of(j * small, small)
        kc = c_ref[0, pl.ds(start, tk), :]
        kT = jnp.concatenate([kT_ref[0, j + i] for i in range(n_small)], axis=1)
        score = lambda i: _dot(q_scr[i * sub_rows:(i + 1) * sub_rows, :], kT)
        s_next = score(0)
        for i in range(n_sub):
            s = s_next
            if i + 1 < n_sub:
                s_next = score(i + 1)
            if masked:
                row = i * sub_rows + lax.broadcasted_iota(jnp.int32, (sub_rows, tk), 0)
                key = start + lax.broadcasted_iota(jnp.int32, (sub_rows, tk), 1)
                s = jnp.where(key <= qi * tq + (row & (tq - 1)), s, NEG_INF)
            _softmax_update(s, kc, m_scr, l_scr, acc_scr, slice(i * sub_rows, (i + 1) * sub_rows))

    n_full = (qi * tq + 1) // small
    n_total = (qi * tq + tq - 1) // small + 1
    done = 0
    for size in levels:
        count = (n_full - done) // size

        def full_body(j, carry, size=size, done=done):
            chunk(done + j * size, size, False)
            return carry

        lax.fori_loop(0, count, full_body, 0)
        done = done + count * size
    n_rest = n_total - done
    for size in range(1, levels[-1] + 1):
        def rest_body(j, carry, size=size):
            chunk(done, size, True)
            return carry

        lax.fori_loop(0, (n_rest == size).astype(jnp.int32), rest_body, 0)
    inv = 1.0 / l_scr[...]
    o = acc_scr[...] * jnp.tile(inv, (1, kv_lora // LANES))
    for h in range(heads):
        o_ref[0, :, h * kv_lora:(h + 1) * kv_lora] = o[h * tq:(h + 1) * tq].astype(o_ref.dtype)


def _attn_prompt(q, c_kv, k_t, heads):
    B, T, C = c_kv.shape
    qk_w = C + LANES
    tq, small, levels = ATTN_Q_TOKENS, ATTN_K_SMALL, ATTN_K_LEVELS
    assert T % small == 0 and small % tq == 0 and tq & (tq - 1) == 0 and C % LANES == 0
    assert all(a % b == 0 for a, b in zip(levels, levels[1:]))
    rows = tq * heads
    kern = functools.partial(_attn_prompt_kernel, heads=heads, tq=tq, small=small, levels=levels, kv_lora=C,
                             sub_rows=math.gcd(rows, ATTN_SUB_ROWS))
    return pl.pallas_call(
        kern,
        grid=(B, T // tq),
        in_specs=[pl.BlockSpec((1, tq, heads * qk_w), lambda b, i: (b, i, 0)),
                  pl.BlockSpec((1, T, C), lambda b, i: (b, 0, 0)),
                  pl.BlockSpec((1, T // small, qk_w, small), lambda b, i: (b, 0, 0, 0))],
        out_specs=pl.BlockSpec((1, tq, heads * C), lambda b, i: (b, i, 0)),
        out_shape=jax.ShapeDtypeStruct((B, T, heads * C), BF16),
        scratch_shapes=[pltpu.VMEM((rows, qk_w), BF16),
                        pltpu.VMEM((rows, LANES), F32), pltpu.VMEM((rows, LANES), F32),
                        pltpu.VMEM((rows, C), F32)],
        compiler_params=_cparams("parallel", "arbitrary"),
        name="attn_prompt",
    )(q, c_kv, k_t)


def _attn_sample_kernel(pt_ref, ql_ref, qp_ref, cn_ref, knT_ref, cache_c, cache_kT, o_ref,
                        cbuf, kbuf, sem, m_scr, l_scr, acc_scr,
                        *, layer, group, n_groups, total, slots, parts, page, heads):
    b = pl.program_id(0)

    def copies(gid):
        slot = gid % slots
        out = []
        for g in range(group):
            pg = pt_ref[gid * group + g]
            out.append(pltpu.make_async_copy(cache_c.at[layer, pg], cbuf.at[slot, pl.ds(g * page, page), :],
                                             sem.at[0, slot]))
            out.append(pltpu.make_async_copy(cache_kT.at[layer, pg], kbuf.at[slot, :, pl.ds(g * page, page)],
                                             sem.at[1, slot]))
        return out

    @pl.when(b == 0)
    def _():
        for g0 in range(min(slots - 1, total)):
            for cp in copies(g0):
                cp.start()

    m_scr[...] = jnp.full(m_scr.shape, -jnp.inf, F32)
    l_scr[...] = jnp.zeros(l_scr.shape, F32)
    acc_scr[...] = jnp.zeros(acc_scr.shape, F32)
    ql = ql_ref[0]
    qp = qp_ref[0]

    def body(j, carry):
        gid = b * n_groups + j

        @pl.when(gid + slots - 1 < total)
        def _():
            for cp in copies(gid + slots - 1):
                cp.start()

        for cp in copies(gid):
            cp.wait()
        slot = gid % slots
        part = group * page // parts
        kcs = [cbuf[slot, i * part:(i + 1) * part, :].astype(BF16) for i in range(parts)]
        kTs = [kbuf[slot, :, i * part:(i + 1) * part].astype(BF16) for i in range(parts)]
        scores = [_dot_nt(ql, kcs[i]) + _dot(qp, kTs[i]) for i in range(parts)]
        for i in range(parts):
            _softmax_update(scores[i], kcs[i], m_scr, l_scr, acc_scr)
        return carry

    lax.fori_loop(0, n_groups, body, 0)

    cn = cn_ref[0]
    s = _dot_nt(ql, cn) + _dot(qp, knT_ref[0])
    rows = s.shape[0]
    tok = lax.broadcasted_iota(jnp.int32, (rows, LANES), 0) // heads
    key = lax.broadcasted_iota(jnp.int32, (rows, LANES), 1)
    _softmax_update(jnp.where(key <= tok, s, NEG_INF), cn, m_scr, l_scr, acc_scr)
    inv = 1.0 / l_scr[...]
    o_ref[0] = (acc_scr[...] * jnp.tile(inv, (1, acc_scr.shape[1] // LANES))).astype(o_ref.dtype)


def _attn_sample(q_lat, q_pe, c_new, k_new_t, cache_c, cache_kT, page_table, layer, heads):
    DB, rows, C = q_lat.shape
    R = cache_kT.shape[2]
    page = cache_c.shape[2]
    n_pages = page_table.shape[1]
    assert page == LANES and rows // heads <= LANES
    group = math.gcd(PAGES_PER_GROUP, n_pages)
    n_groups = n_pages // group
    pt = page_table.reshape(-1).astype(jnp.int32)
    per_b = lambda shape: pl.BlockSpec((1,) + shape, lambda b, pt_ref: (b, 0, 0))
    slots = SAMPLE_SLOTS
    kern = functools.partial(_attn_sample_kernel, layer=layer, group=group, n_groups=n_groups,
                             total=DB * n_groups, slots=slots, parts=math.gcd(SAMPLE_PARTS, group), page=page,
                             heads=heads)
    grid_spec = pltpu.PrefetchScalarGridSpec(
        num_scalar_prefetch=1,
        grid=(DB,),
        in_specs=[per_b((rows, C)), per_b((rows, R)), per_b((LANES, C)), per_b((R, LANES)),
                  pl.BlockSpec(memory_space=pl.ANY), pl.BlockSpec(memory_space=pl.ANY)],
        out_specs=per_b((rows, C)),
        scratch_shapes=[pltpu.VMEM((slots, group * page, C), F32), pltpu.VMEM((slots, R, group * page), F32),
                        pltpu.SemaphoreType.DMA((2, slots)),
                        pltpu.VMEM((rows, LANES), F32), pltpu.VMEM((rows, LANES), F32),
                        pltpu.VMEM((rows, C), F32)],
    )
    return pl.pallas_call(
        kern,
        grid_spec=grid_spec,
        out_shape=jax.ShapeDtypeStruct((DB, rows, C), BF16),
        compiler_params=_cparams("arbitrary"),
        name="attn_sample",
    )(pt, q_lat, q_pe, c_new, k_new_t, cache_c, cache_kT)


def _mix_ffn_kernel(x_ref, p_ref, q_ref, wo_ref, g_ref, wup_ref, wdn_ref, gnext_ref, *rest, mla_heads, emit_h):
    outs, (xn_scr, acc_scr, *cat_scr) = rest[:1 + emit_h], rest[1 + emit_h:]
    nnext_ref = outs[-1]
    k = pl.program_id(2)

    @pl.when(k == 0)
    def _():
        if mla_heads:
            kv_lora, vhead = q_ref.shape[1:]
            for h in range(mla_heads):
                oh = _dot(p_ref[:, h * kv_lora:(h + 1) * kv_lora], q_ref[h])
                cat_scr[0][:, h * vhead:(h + 1) * vhead] = oh.astype(BF16)
            mixed = cat_scr[0][...]
        else:
            mixed = (p_ref[...] * q_ref[...]).astype(BF16)
        h = x_ref[...] + _dot(mixed, wo_ref[...])
        xn_scr[...] = _rms(h, g_ref[...]).astype(BF16)
        acc_scr[...] = h

    u = jnp.maximum(_dot(xn_scr[...], wup_ref[...]), 0.0)
    acc_scr[...] += _dot((u * u).astype(BF16), wdn_ref[...])

    @pl.when(k == pl.num_programs(2) - 1)
    def _():
        out = acc_scr[...]
        if emit_h:
            outs[0][...] = out
        nnext_ref[...] = _rms(out, gnext_ref[...])


def _mix_ffn(x, p, q, wo, g, wup, wdn, g_next, *, mla_heads=0, emit_h=True, window=None):
    N, D = x.shape
    FF = wup.shape[1]
    stride, offset, length = (N, 0, N) if window is None else window
    n_seq = N // stride
    tf = min(FF_TILE, FF)
    if offset == 0 and length == stride:
        tm = _row_tile(length)
        rows = lambda w: pl.BlockSpec((tm, w), lambda b, i, k: (b * (stride // tm) + i, 0))
    else:
        tm = _row_tile(length, FFN_WINDOW_TILE)
        assert stride % 8 == 0 and offset % 8 == 0 and tm % 8 == 0
        rows = lambda w: pl.BlockSpec((pl.Element(tm), pl.Element(w)),
                                      lambda b, i, k: (pl.multiple_of(b * stride + offset + i * tm, 8), 0))
    const = lambda arr: pl.BlockSpec(arr.shape, lambda b, i, k: (0,) * arr.ndim)
    out_spec = pl.BlockSpec((tm, D), lambda b, i, k: (b * (length // tm) + i, 0))
    out_sds = jax.ShapeDtypeStruct((n_seq * length, D), F32)
    n_out = 2 if emit_h else 1
    scratch = [pltpu.VMEM((tm, D), BF16), pltpu.VMEM((tm, D), F32)]
    if mla_heads:
        scratch.append(pltpu.VMEM((tm, wo.shape[0]), BF16))
    res = pl.pallas_call(
        functools.partial(_mix_ffn_kernel, mla_heads=mla_heads, emit_h=emit_h),
        grid=(n_seq, length // tm, FF // tf),
        in_specs=[rows(D), rows(p.shape[1]), const(q) if mla_heads else rows(D), const(wo), const(g),
                  pl.BlockSpec((D, tf), lambda b, i, k: (0, k)),
                  pl.BlockSpec((tf, D), lambda b, i, k: (k, 0)),
                  const(g_next)],
        out_specs=[out_spec] * n_out,
        out_shape=[out_sds] * n_out,
        scratch_shapes=scratch,
        compiler_params=_cparams("parallel", "parallel", "arbitrary"),
        name="mix_ffn",
    )(x, p, q, wo, g, wup, wdn, g_next)
    return res if emit_h else (None, res[0])


def _rwkv_proj_kernel(n_ref, xp_ref, mu_ref, wr_ref, wk_ref, wv_ref, dw0_ref, dw1_ref, dw2_ref,
                      aw0_ref, aw1_ref, aw2_ref, gw1_ref, gw2_ref,
                      r_ref, k_ref, v_ref, d_ref, a_ref, g_ref, *, seq_len):
    n = n_ref[...]
    if seq_len is None:
        x_prev = xp_ref[...]
    else:
        tm = n.shape[0]
        inside = ((pl.program_id(0) * tm) % seq_len != 0).astype(F32)
        row = lax.broadcasted_iota(jnp.int32, (tm, 1), 0)
        x_prev = jnp.where(row == 0, xp_ref[7:8, :] * inside, pltpu.roll(n, 1, axis=0))
    xx = x_prev - n
    mix = lambda j: (n + xx * mu_ref[j:j + 1, :]).astype(BF16)
    r_ref[...] = _dot(mix(0), wr_ref[...])
    k_ref[...] = _dot(mix(2), wk_ref[...])
    v_ref[...] = _dot(mix(3), wv_ref[...])
    z = dw0_ref[...] + _dot(jnp.tanh(_dot(mix(1), dw1_ref[...])).astype(BF16), dw2_ref[...])
    d_ref[...] = (-math.exp(-0.5)) / (1.0 + jnp.exp(-z))
    za = aw0_ref[...] + _dot(_dot(mix(4), aw1_ref[...]).astype(BF16), aw2_ref[...])
    a_ref[...] = 1.0 / (1.0 + jnp.exp(-za))
    zg = _dot(mix(5), gw1_ref[...])
    g_ref[...] = _dot((1.0 / (1.0 + jnp.exp(-zg))).astype(BF16), gw2_ref[...])


def _rwkv_project(n, x_prev, wts, seq_len=None):
    N, D = n.shape
    tm = _row_tile(N if seq_len is None else seq_len, RWKV_ROW_TILE)
    row = pl.BlockSpec((tm, D), lambda i: (i, 0))
    full = lambda arr: pl.BlockSpec(arr.shape, lambda i: (0,) * arr.ndim)
    if seq_len is None:
        prev_spec = row
    else:
        x_prev = n
        prev_spec = pl.BlockSpec((8, D), lambda i: (jnp.maximum(i * (tm // 8) - 1, 0), 0))
    return pl.pallas_call(
        functools.partial(_rwkv_proj_kernel, seq_len=seq_len),
        grid=(N // tm,),
        in_specs=[row, prev_spec] + [full(w) for w in wts],
        out_specs=[row] * 6,
        out_shape=[jax.ShapeDtypeStruct((N, D), F32)] * 6,
        compiler_params=_cparams("parallel"),
        name="rwkv_project",
    )(n, x_prev, *wts)


def _wkv_kernel(r_ref, k_ref, v_ref, d_ref, a_ref, kk_ref, ka_ref, rk_ref, lnw_ref, lnb_ref, s0_ref,
                y_ref, sout_ref, st_scr, *, L, t_valid, nb, pairs, hd):
    c = pl.program_id(0)
    L2 = 2 * L
    W = 2 * hd
    P = range(nb * pairs)
    bi = [q // pairs for q in P]
    hp = [q % pairs for q in P]
    m0 = lax.broadcasted_iota(jnp.int32, (1, W), 1) < hd

    @pl.when(c == 0)
    def _():
        z = jnp.zeros((hd, hd), F32)
        for p in P:
            s_bd = jnp.concatenate([jnp.concatenate([s0_ref[bi[p], 2 * hp[p]], z], axis=1),
                                    jnp.concatenate([z, s0_ref[bi[p], 2 * hp[p] + 1]], axis=1)], axis=0)
            st_scr[p] = s_bd.T

    valid = (c * L + lax.broadcasted_iota(jnp.int32, (L, 1), 0)) < t_valid
    tri = (lax.broadcasted_iota(jnp.int32, (L, L), 0) >= lax.broadcasted_iota(jnp.int32, (L, L), 1))
    tri3 = jnp.tile(tri.astype(BF16), (1, 3))
    ones3 = jnp.ones((3 * L, W), BF16)
    i2 = lax.broadcasted_iota(jnp.int32, (L2, L2), 0)
    j2 = lax.broadcasted_iota(jnp.int32, (L2, L2), 1)
    mask_s = j2 < i2
    mask_i = j2 <= i2
    eye = (i2 == j2).astype(F32)

    def stack(x):
        return jnp.concatenate([jnp.where(m0, x, 0.0), jnp.where(m0, 0.0, x)], axis=0).astype(BF16)

    def head_sum(x):
        s_a = jnp.sum(jnp.where(m0, x, 0.0), axis=1, keepdims=True)
        s_b = jnp.sum(jnp.where(m0, 0.0, x), axis=1, keepdims=True)
        return jnp.where(m0, s_a, s_b)

    def split3(x):
        hi = x.astype(BF16)
        r1 = x - hi.astype(F32)
        mid = r1.astype(BF16)
        lo = (r1 - mid.astype(F32)).astype(BF16)
        return jnp.concatenate([hi, mid, lo], axis=0)

    sl = [slice(hp[p] * W, (hp[p] + 1) * W) for p in P]
    load = lambda ref, p: jnp.where(valid, ref[bi[p], :, sl[p]], 0.0)
    r = [load(r_ref, p) for p in P]
    k = [load(k_ref, p) for p in P]
    v = [load(v_ref, p) for p in P]
    d = [load(d_ref, p) for p in P]
    a = [load(a_ref, p) for p in P]
    d3 = [split3(d[p]) for p in P]
    cum = [_dot(tri3, d3[p]) for p in P]
    kk = [k[p] * kk_ref[:, sl[p]] for p in P]
    kk = [kk[p] * lax.rsqrt(jnp.maximum(head_sum(kk[p] * kk[p]), KK_NORM_EPS ** 2)) for p in P]
    kp = [k[p] * (1.0 + (a[p] - 1.0) * ka_ref[:, sl[p]]) for p in P]
    bv = [kk[p] * a[p] for p in P]
    e_neg = [jnp.exp(-cum[p]) for p in P]
    a_st = [stack(-kk[p] * jnp.exp(cum[p] - d[p])) for p in P]
    r_st = [stack(r[p] * jnp.exp(cum[p])) for p in P]
    b_st = [stack(bv[p] * e_neg[p]) for p in P]
    k_st = [stack(kp[p] * e_neg[p]) for p in P]
    v_st = [stack(v[p]) for p in P]
    pm = [_dot_nt(jnp.concatenate([a_st[p], r_st[p]], axis=0), jnp.concatenate([b_st[p], k_st[p]], axis=0))
          for p in P]
    m_ab = [jnp.where(mask_s, pm[p][:L2, :L2], 0.0) for p in P]
    m_ak = [jnp.where(mask_s, pm[p][:L2, L2:], 0.0).astype(BF16) for p in P]
    m_rb = [jnp.where(mask_i, pm[p][L2:, :L2], 0.0).astype(BF16) for p in P]
    m_rk = [jnp.where(mask_i, pm[p][L2:, L2:], 0.0).astype(BF16) for p in P]
    st = [st_scr[p] for p in P]
    stb = [st[p].astype(BF16) for p in P]
    rhs = [_dot(jnp.concatenate([a_st[p], m_ak[p]], axis=1), jnp.concatenate([stb[p], v_st[p]], axis=0)) for p in P]
    inv = [eye + m_ab[p] for p in P]
    pw = [m_ab[p].astype(BF16) for p in P]
    pw = [_dot(pw[p], pw[p]).astype(BF16) for p in P]
    levels = int(math.log2(L))
    for lvl in range(1, levels):
        last = lvl == levels - 1
        z = [_dot(pw[p], inv[p].astype(BF16) if last else jnp.concatenate([inv[p].astype(BF16), pw[p]], axis=1))
             for p in P]
        inv = [inv[p] + z[p][:, :L2] for p in P]
        if not last:
            pw = [z[p][:, L2:].astype(BF16) for p in P]
    u_st = [_dot(inv[p].astype(BF16), rhs[p].astype(BF16)).astype(BF16) for p in P]
    y_st = [_dot(jnp.concatenate([r_st[p], m_rb[p], m_rk[p]], axis=1),
                 jnp.concatenate([stb[p], u_st[p], v_st[p]], axis=0)) for p in P]
    decay_col = [jnp.exp(_dot_tn(d3[p], ones3)) for p in P]
    e_rest = [jnp.exp(cum[p][L - 1:L, :] - cum[p]) for p in P]
    for p in P:
        st_scr[p] = decay_col[p] * st[p] + _dot_tn(
            jnp.concatenate([stack(bv[p] * e_rest[p]), stack(kp[p] * e_rest[p])], axis=0),
            jnp.concatenate([u_st[p], v_st[p]], axis=0))
    for p in P:
        y = y_st[p][:L] + y_st[p][L:]
        mean = head_sum(y) * (1.0 / hd)
        yc = y - mean
        var = head_sum(yc * yc) * (1.0 / hd)
        yn = yc * lax.rsqrt(var + GN_EPS) * lnw_ref[:, sl[p]] + lnb_ref[:, sl[p]]
        y_ref[bi[p], :, sl[p]] = yn + head_sum(r[p] * kp[p] * rk_ref[:, sl[p]]) * v[p]

    @pl.when(c == pl.num_programs(0) - 1)
    def _():
        for p in P:
            s_bd = st_scr[p].T
            sout_ref[bi[p], 2 * hp[p]] = s_bd[:hd, :hd]
            sout_ref[bi[p], 2 * hp[p] + 1] = s_bd[hd:, hd:]


def _wkv(r, k, v, d, a, params, s0, L, t_valid):
    B, T, D = r.shape
    H, hd = s0.shape[1], s0.shape[2]
    assert 2 * hd == LANES and H % 2 == 0 and T % L == 0 and (2 * L) % LANES == 0
    pairs = H // 2
    seq = pl.BlockSpec((B, L, D), lambda c: (0, c, 0))
    vec = pl.BlockSpec((1, D), lambda c: (0, 0))
    state = pl.BlockSpec((B, H, hd, hd), lambda c: (0, 0, 0, 0))
    kern = functools.partial(_wkv_kernel, L=L, t_valid=t_valid, nb=B, pairs=pairs, hd=hd)
    return pl.pallas_call(
        kern,
        grid=(T // L,),
        in_specs=[seq] * 5 + [vec] * 5 + [state],
        out_specs=[seq, state],
        out_shape=[jax.ShapeDtypeStruct((B, T, D), F32), jax.ShapeDtypeStruct(s0.shape, F32)],
        scratch_shapes=[pltpu.VMEM((B * pairs, LANES, LANES), F32)],
        compiler_params=_cparams("arbitrary"),
        name="wkv",
    )(r, k, v, d, a, *params, s0)


def _wkv_sample_kernel(r_ref, k_ref, v_ref, d_ref, a_ref, kk_ref, ka_ref, rk_ref, lnw_ref, lnb_ref, s0_ref,
                       y_ref, sout_ref, w_scr, a_scr, b_scr, kp_scr, yraw_scr, *, steps, hd):
    for t in range(steps):
        kt = k_ref[t]
        at = a_ref[t]
        kk = kt * kk_ref[...]
        kk = kk * lax.rsqrt(jnp.maximum(jnp.sum(kk * kk, axis=0, keepdims=True), KK_NORM_EPS ** 2))
        w_scr[t] = jnp.exp(d_ref[t])
        a_scr[t] = -kk
        b_scr[t] = kk * at
        kp_scr[t] = kt * (1.0 + (at - 1.0) * ka_ref[...])

    def body(i, carry):
        rows = [i * WKV_SAMPLE_ROWS + j for j in range(WKV_SAMPLE_ROWS)]
        s = [s0_ref[vi] for vi in rows]
        for t in range(steps):
            sa = [jnp.sum(s[j] * a_scr[t], axis=0, keepdims=True) for j in range(len(rows))]
            s = [s[j] * w_scr[t] + sa[j] * b_scr[t] + v_ref[t, pl.ds(vi, 1), :] * kp_scr[t]
                 for j, vi in enumerate(rows)]
            for j, vi in enumerate(rows):
                yraw_scr[t, pl.ds(vi, 1), :] = jnp.sum(s[j] * r_ref[t], axis=0, keepdims=True)
        for j, vi in enumerate(rows):
            sout_ref[vi] = s[j]
        return carry

    assert hd % WKV_SAMPLE_ROWS == 0
    lax.fori_loop(0, hd // WKV_SAMPLE_ROWS, body, 0)
    for t in range(steps):
        y = yraw_scr[t]
        mean = jnp.mean(y, axis=0, keepdims=True)
        yc = y - mean
        var = jnp.mean(yc * yc, axis=0, keepdims=True)
        yn = yc * lax.rsqrt(var + GN_EPS) * lnw_ref[...] + lnb_ref[...]
        bonus = jnp.sum(r_ref[t] * kp_scr[t] * rk_ref[...], axis=0, keepdims=True)
        y_ref[t] = yn + bonus * v_ref[t]


def _wkv_sample(r, k, v, d, a, params, s0):
    S, D, DB = r.shape
    H, hd = s0.shape[0], s0.shape[1]
    seq = pl.BlockSpec((S, hd, DB), lambda h: (0, h, 0))
    vec = pl.BlockSpec((hd, DB), lambda h: (h, 0))
    state = pl.BlockSpec((None, hd, hd, DB), lambda h: (h, 0, 0, 0))
    kern = functools.partial(_wkv_sample_kernel, steps=S, hd=hd)
    return pl.pallas_call(
        kern,
        grid=(H,),
        in_specs=[seq] * 5 + [vec] * 5 + [state],
        out_specs=[seq, state],
        out_shape=[jax.ShapeDtypeStruct((S, D, DB), F32), jax.ShapeDtypeStruct(s0.shape, F32)],
        scratch_shapes=[pltpu.VMEM((S, hd, DB), F32)] * 5,
        compiler_params=_cparams("parallel"),
        name="wkv_sample",
    )(r, k, v, d, a, *params, s0)


def _rope_tables(pos, rope, heads):
    half = rope // 2
    inv_freq = ROPE_THETA ** (-jnp.arange(half, dtype=F32) / half)
    ang = pos.astype(F32)[:, None] * inv_freq[None, :]
    cos, sin = jnp.cos(ang), jnp.sin(ang)
    cos2 = jnp.concatenate([cos, cos], axis=-1)
    sin2 = jnp.concatenate([-sin, sin], axis=-1)
    pad = lambda t: jnp.pad(t, ((0, 0), (0, LANES - rope)))
    return pad(cos2), pad(sin2), cos2.T, sin2.T


def _swap_halves(w, width):
    lead = w.shape[:-1]
    g = w.reshape(lead + (-1, 2, width // 2))
    return g[..., ::-1, :].reshape(w.shape)


def _pad_heads(w, heads, rope):
    K = w.shape[0]
    return jnp.pad(w.reshape(K, heads, rope), ((0, 0), (0, 0), (0, LANES - rope))).reshape(K, heads * LANES)


def kernel(x_prompt, x_sample, cache_kv_latent, cache_k_rope, state_wkv, state_shift, page_table, meta_tokens, norm_mix, norm_ffn, norm_final, mla_w_qkv_a, mla_q_a_norm, mla_kv_a_norm, mla_w_q_b, mla_w_kv_b, mla_w_o, rwkv_mu, rwkv_w_r, rwkv_w_k, rwkv_w_v, rwkv_w_o, rwkv_decay_w0, rwkv_decay_w1, rwkv_decay_w2, rwkv_a_w0, rwkv_a_w1, rwkv_a_w2, rwkv_g_w1, rwkv_g_w2, rwkv_k_k, rwkv_k_a, rwkv_r_k, rwkv_ln_w, rwkv_ln_b, ffn_w_up, ffn_w_down):
    B, SEQ, D = x_prompt.shape
    DB, S, _ = x_sample.shape
    n_meta = meta_tokens.shape[0]
    T = n_meta + SEQ
    Tp = -(-T // SEQ_ALIGN) * SEQ_ALIGN
    page = cache_kv_latent.shape[2]
    past_len = page_table.shape[1] * page
    kv_lora = cache_kv_latent.shape[-1]
    rope = cache_k_rope.shape[-1]
    q_lora = mla_q_a_norm.shape[-1]
    H, hd = state_wkv.shape[2], state_wkv.shape[3]
    qk = mla_w_q_b.shape[-1]
    kvb = mla_w_kv_b.shape[-1]
    ov = mla_w_o.shape[1]
    heads = (qk + ov - kvb) // rope
    nope = qk // heads - rope
    vhead = ov // heads
    qk_w = kv_lora + LANES
    scale = float(nope + rope) ** -0.5 * LOG2E
    dims = (q_lora, kv_lora, rope, heads, nope, scale)

    row = lambda vec: vec.reshape(1, -1).astype(F32)
    b16 = lambda w: w.astype(BF16)

    meta = jnp.broadcast_to(meta_tokens.astype(F32)[None], (B, n_meta, D))
    hp = jnp.concatenate([meta, x_prompt, jnp.zeros((B, Tp - T, D), F32)], axis=1)
    hs = x_sample.reshape(1, DB * S, D)

    l = 0
    wqkv = mla_w_qkv_a[l]
    wa = b16(wqkv[:, :q_lora + kv_lora])
    wkvT = b16(wqkv[:, q_lora:q_lora + kv_lora].T)
    wk = wqkv[:, q_lora + kv_lora:]
    wkT = b16(jnp.concatenate([wk, _swap_halves(wk, rope)], axis=1).T)
    wqb = mla_w_q_b[l].reshape(q_lora, heads, nope + rope)
    wnope = b16(wqb[..., :nope].reshape(q_lora, heads * nope))
    wpe = wqb[..., nope:].reshape(q_lora, heads * rope)
    wpesw = b16(_pad_heads(_swap_halves(wpe, rope), heads, rope))
    wpe = b16(_pad_heads(wpe, heads, rope))
    wkvb = mla_w_kv_b[l].reshape(kv_lora, heads, nope + vhead)
    wuk = b16(jnp.transpose(wkvb[..., :nope], (1, 2, 0)))
    wuv = b16(jnp.transpose(wkvb[..., nope:], (1, 0, 2)))
    wo = b16(mla_w_o[l])
    kvn = mla_kv_a_norm[l].astype(F32)
    mla_w = (wa, wkvT, wkT, row(mla_q_a_norm[l]), row(kvn), kvn.reshape(-1, 1), wnope, wpe, wpesw, wuk)

    tabs_p = _rope_tables(jnp.arange(Tp), rope, heads)
    tabs_s = _rope_tables(jnp.tile(past_len + jnp.arange(S), DB), rope, heads)

    q_p, c_p, kpeT_p, cb_p, kT_p = _mla_project(hp, row(norm_mix[0]), mla_w, tabs_p, dims)
    q_s, c_s, kpeT_s, cb_s, kT_s = _mla_project(hs, row(norm_mix[0]), mla_w, tabs_s, dims)

    ks = ATTN_K_SMALL
    kT_chunks = kT_p.reshape(B, qk_w, Tp // ks, ks).transpose(0, 2, 1, 3)
    o_p = _attn_prompt(q_p, cb_p, kT_chunks, heads)

    q_s = q_s.reshape(DB, S * heads, qk_w)
    c_new = jnp.pad(cb_s.reshape(DB, S, kv_lora), ((0, 0), (0, LANES - S), (0, 0)))
    k_new_t = kT_s[0, kv_lora:kv_lora + rope].reshape(rope, DB, S).transpose(1, 0, 2)
    k_new_t = jnp.pad(k_new_t, ((0, 0), (0, 0), (0, LANES - S)))
    o_s = _attn_sample(q_s[..., :kv_lora], q_s[..., kv_lora:kv_lora + rope], c_new, k_new_t,
                       cache_kv_latent, jnp.swapaxes(cache_k_rope, 2, 3), page_table, l, heads)

    wup0, wdn0 = b16(ffn_w_up[0]), b16(ffn_w_down[0])
    hp, np_ = _mix_ffn(hp.reshape(B * Tp, D), o_p.reshape(B * Tp, heads * kv_lora), wuv, wo,
                       row(norm_ffn[0]), wup0, wdn0, row(norm_mix[1]), mla_heads=heads)
    hs, ns_ = _mix_ffn(hs.reshape(DB * S, D), o_s.reshape(DB * S, heads * kv_lora), wuv, wo,
                       row(norm_ffn[0]), wup0, wdn0, row(norm_mix[1]), mla_heads=heads)

    np3 = np_.reshape(B, Tp, D)
    ns3 = ns_.reshape(DB, S, D)
    xprev_s = jnp.concatenate([state_shift[l].astype(F32)[:, None], ns3[:, :-1]], axis=1).reshape(DB * S, D)
    rw = (rwkv_mu[l].astype(F32), b16(rwkv_w_r[l]), b16(rwkv_w_k[l]), b16(rwkv_w_v[l]),
          row(rwkv_decay_w0[l]), b16(rwkv_decay_w1[l]), b16(rwkv_decay_w2[l]),
          row(rwkv_a_w0[l]), b16(rwkv_a_w1[l]), b16(rwkv_a_w2[l]),
          b16(rwkv_g_w1[l]), b16(rwkv_g_w2[l]))
    par = (rwkv_k_k[l], rwkv_k_a[l], rwkv_r_k[l], rwkv_ln_w[l], rwkv_ln_b[l])

    r_p, k_p, v_p, d_p, a_p, g_p = _rwkv_project(np_, None, rw, seq_len=Tp)
    r_s, k_s, v_s, d_s, a_s, g_s = _rwkv_project(ns_, xprev_s, rw)

    seq_p = lambda t: t.reshape(B, Tp, D)
    y_p, st_p = _wkv(seq_p(r_p), seq_p(k_p), seq_p(v_p), seq_p(d_p), seq_p(a_p), tuple(row(x) for x in par),
                     jnp.zeros((B, H, hd, hd), F32), WKV_CHUNK, T)

    lanes_b = lambda t: t.reshape(DB, S, D).transpose(1, 2, 0)
    par_b = tuple(jnp.broadcast_to(x.reshape(D, 1).astype(F32), (D, DB)) for x in par)
    y_s, st_s = _wkv_sample(lanes_b(r_s), lanes_b(k_s), lanes_b(v_s), lanes_b(d_s), lanes_b(a_s), par_b,
                            jnp.transpose(state_wkv[l].astype(F32), (1, 2, 3, 0)))
    y_s = y_s.transpose(2, 0, 1).reshape(DB * S, D)
    st_s = jnp.transpose(st_s, (3, 0, 1, 2))

    wo_r = b16(rwkv_w_o[l])
    wup1, wdn1 = b16(ffn_w_up[1]), b16(ffn_w_down[1])
    _, yp = _mix_ffn(hp, y_p.reshape(B * Tp, D), g_p, wo_r, row(norm_ffn[1]), wup1, wdn1, row(norm_final),
                     emit_h=False, window=(Tp, n_meta, SEQ))
    _, ys = _mix_ffn(hs, y_s, g_s, wo_r, row(norm_ffn[1]), wup1, wdn1, row(norm_final), emit_h=False)

    y_prompt = yp.reshape(B, SEQ, D)
    y_sample = ys.reshape(DB, S, D)
    k_rope_p = jnp.swapaxes(kpeT_p[:, :, :T], 1, 2)
    k_rope_s = kpeT_s[0].reshape(rope, DB, S).transpose(1, 2, 0)
    return (y_prompt, y_sample,
            c_p[None, :, :T], k_rope_p[None],
            c_s.reshape(1, DB, S, kv_lora), k_rope_s[None],
            st_p[None], np3[None, :, T - 1], st_s[None], ns3[None, :, S - 1])
```

```python
import functools
import math

import jax
import jax.numpy as jnp
from jax import lax
from jax.experimental import pallas as pl
from jax.experimental.pallas import tpu as pltpu

F32 = jnp.float32
BF16 = jnp.bfloat16

RMS_EPS = 1e-6
GN_EPS = 64e-5
KK_NORM_EPS = 1e-12
NEG_INF = -1e30
ROPE_THETA = 10000.0
LOG2E = 1.4426950408889634

LANES = 128
VMEM_LIMIT_BYTES = 56 * 1024 * 1024

ATTN_Q_TOKENS = 256
ATTN_K_SMALL = 256
ATTN_K_LEVELS = (8, 4, 2)
ATTN_SUB_ROWS = 256
SEQ_ALIGN = 256
ROW_TILE = 768
RWKV_ROW_TILE = 384
FF_TILE = 1024
FFN_WINDOW_TILE = 1024
WKV_CHUNK = 64
WKV_SAMPLE_ROWS = 2
PAGES_PER_GROUP = 16
SAMPLE_SLOTS = 3
SAMPLE_PARTS = 4


def _cparams(*sem):
    return pltpu.CompilerParams(dimension_semantics=sem, vmem_limit_bytes=VMEM_LIMIT_BYTES)


def _dot(a, b):
    return jnp.dot(a, b, preferred_element_type=F32)


def _dot_nt(a, b):
    return lax.dot_general(a, b, (((1,), (1,)), ((), ())), preferred_element_type=F32)


def _dot_tn(a, b):
    return lax.dot_general(a, b, (((0,), (0,)), ((), ())), preferred_element_type=F32)


def _rms(x, g):
    return x * lax.rsqrt(jnp.mean(x * x, axis=-1, keepdims=True) + RMS_EPS) * g


def _row_tile(n, target=ROW_TILE, align=8):
    best = None
    for t in range(align, min(n, target) + 1, align):
        if n % t == 0:
            best = t
    assert best is not None, n
    return best


def _mla_proj_kernel(x_ref, g_ref, wa_ref, wkvT_ref, wkT_ref, qn_ref, kvn_ref, kvnc_ref, wnope_ref, wpe_ref,
                     wpesw_ref, wuk_ref, cosq_ref, sinq_ref, cosk_ref, sink_ref,
                     q_ref, ckv_ref, kpeT_ref, ckvb_ref, kT_ref,
                     *, q_lora, kv_lora, rope, heads, nope, scale):
    tm = x_ref.shape[1]
    qk_w = kv_lora + LANES
    n = _rms(x_ref[0], g_ref[...]).astype(BF16)
    a = _dot(n, wa_ref[...])
    c_q = _rms(a[:, :q_lora], qn_ref[...]).astype(BF16)
    c_kv = _rms(a[:, q_lora:], kvn_ref[...])
    ckv_ref[0] = c_kv
    ckvb_ref[0] = c_kv.astype(BF16)
    a_kv_t = _dot_nt(wkvT_ref[...], n)
    c_kv_t = a_kv_t * lax.rsqrt(jnp.mean(a_kv_t * a_kv_t, axis=0, keepdims=True) + RMS_EPS) * kvnc_ref[...]
    a_t = _dot_nt(wkT_ref[...], n)
    k_pe_t = a_t[:rope] * cosk_ref[...] + a_t[rope:] * sink_ref[...]
    kpeT_ref[0] = k_pe_t
    kT_ref[0] = jnp.concatenate([c_kv_t.astype(BF16), k_pe_t.astype(BF16),
                                 jnp.zeros((LANES - rope, tm), BF16)], axis=0)
    cosq = jnp.tile(cosq_ref[...], (1, heads))
    sinq = jnp.tile(sinq_ref[...], (1, heads))
    q_pe = _dot(c_q, wpe_ref[...]) * cosq + _dot(c_q, wpesw_ref[...]) * sinq
    q_pe = (q_pe * scale).astype(BF16)
    q_nope = _dot(c_q, wnope_ref[...]).astype(BF16)
    for h in range(heads):
        q_lat = _dot(q_nope[:, h * nope:(h + 1) * nope], wuk_ref[h])
        q_ref[0, :, h * qk_w:h * qk_w + kv_lora] = (q_lat * scale).astype(BF16)
        q_ref[0, :, h * qk_w + kv_lora:(h + 1) * qk_w] = q_pe[:, h * LANES:(h + 1) * LANES]


def _mla_project(x, g, wts, tabs, dims):
    B, T, D = x.shape
    q_lora, kv_lora, rope, heads, nope, scale = dims
    qk_w = kv_lora + LANES
    tm = _row_tile(T, align=LANES)
    wa, wkvT, wkT, qn, kvn, kvnc, wnope, wpe, wpesw, wuk = wts
    cosq, sinq, cosk, sink = tabs
    full = lambda arr: pl.BlockSpec(arr.shape, lambda b, i: (0,) * arr.ndim)
    row = lambda w: pl.BlockSpec((1, tm, w), lambda b, i: (b, i, 0))
    col = lambda r: pl.BlockSpec((1, r, tm), lambda b, i: (b, 0, i))
    kern = functools.partial(_mla_proj_kernel, q_lora=q_lora, kv_lora=kv_lora, rope=rope,
                             heads=heads, nope=nope, scale=scale)
    return pl.pallas_call(
        kern,
        grid=(B, T // tm),
        in_specs=[row(D), full(g), full(wa), full(wkvT), full(wkT), full(qn), full(kvn), full(kvnc), full(wnope),
                  full(wpe), full(wpesw), full(wuk),
                  pl.BlockSpec((tm, LANES), lambda b, i: (i, 0)),
                  pl.BlockSpec((tm, LANES), lambda b, i: (i, 0)),
                  pl.BlockSpec((rope, tm), lambda b, i: (0, i)),
                  pl.BlockSpec((rope, tm), lambda b, i: (0, i))],
        out_specs=[row(heads * qk_w), row(kv_lora), col(rope), row(kv_lora), col(qk_w)],
        out_shape=[jax.ShapeDtypeStruct((B, T, heads * qk_w), BF16),
                   jax.ShapeDtypeStruct((B, T, kv_lora), F32),
                   jax.ShapeDtypeStruct((B, rope, T), F32),
                   jax.ShapeDtypeStruct((B, T, kv_lora), BF16),
                   jax.ShapeDtypeStruct((B, qk_w, T), BF16)],
        compiler_params=_cparams("parallel", "parallel"),
        name="mla_project",
    )(x, g, wa, wkvT, wkT, qn, kvn, kvnc, wnope, wpe, wpesw, wuk, cosq, sinq, cosk, sink)


def _softmax_update(s, vals, m_scr, l_scr, acc_scr, rows=slice(None)):
    m_prev = m_scr[rows, :]
    m_next = jnp.maximum(m_prev, jnp.max(s, axis=1, keepdims=True))
    p = jnp.exp2(s - jnp.tile(m_next, (1, s.shape[1] // LANES)))
    alpha = jnp.exp2(m_prev - m_next)
    l_scr[rows, :] = alpha * l_scr[rows, :] + jnp.sum(p, axis=1, keepdims=True)
    acc_scr[rows, :] = (acc_scr[rows, :] * jnp.tile(alpha, (1, acc_scr.shape[1] // LANES))
                        + _dot(p.astype(BF16), vals))
    m_scr[rows, :] = m_next


def _attn_prompt_kernel(q_ref, c_ref, kT_ref, o_ref, q_scr, m_scr, l_scr, acc_scr,
                        *, heads, tq, small, levels, kv_lora, sub_rows):
    qi = pl.program_id(1)
    rows = tq * heads
    qk_w = kv_lora + LANES
    for h in range(heads):
        q_scr[h * tq:(h + 1) * tq, :] = q_ref[0, :, h * qk_w:(h + 1) * qk_w]
    m_scr[...] = jnp.full(m_scr.shape, -jnp.inf, F32)
    l_scr[...] = jnp.zeros(l_scr.shape, F32)
    acc_scr[...] = jnp.zeros(acc_scr.shape, F32)
    n_sub = rows // sub_rows

    def chunk(j, n_small, masked):
        tk = n_small * small
        start = pl.multiple_of(j * small, small)
        kc = c_ref[0, pl.ds(start, tk), :]
        kT = jnp.concatenate([kT_ref[0, j + i] for i in range(n_small)], axis=1)
        score = lambda i: _dot(q_scr[i * sub_rows:(i + 1) * sub_rows, :], kT)
        s_next = score(0)
        for i in range(n_sub):
            s = s_next
            if i + 1 < n_sub:
                s_next = score(i + 1)
            if masked:
                row = i * sub_rows + lax.broadcasted_iota(jnp.int32, (sub_rows, tk), 0)
                key = start + lax.broadcasted_iota(jnp.int32, (sub_rows, tk), 1)
                s = jnp.where(key <= qi * tq + (row & (tq - 1)), s, NEG_INF)
            _softmax_update(s, kc, m_scr, l_scr, acc_scr, slice(i * sub_rows, (i + 1) * sub_rows))

    n_full = (qi * tq + 1) // small
    n_total = (qi * tq + tq - 1) // small + 1
    done = 0
    for size in levels:
        count = (n_full - done) // size

        def full_body(j, carry, size=size, done=done):
            chunk(done + j * size, size, False)
            return carry

        lax.fori_loop(0, count, full_body, 0)
        done = done + count * size
    n_rest = n_total - done
    for size in range(1, levels[-1] + 1):
        def rest_body(j, carry, size=size):
            chunk(done, size, True)
            return carry

        lax.fori_loop(0, (n_rest == size).astype(jnp.int32), rest_body, 0)
    inv = 1.0 / l_scr[...]
    o = acc_scr[...] * jnp.tile(inv, (1, kv_lora // LANES))
    for h in range(heads):
        o_ref[0, :, h * kv_lora:(h + 1) * kv_lora] = o[h * tq:(h + 1) * tq].astype(o_ref.dtype)


def _attn_prompt(q, c_kv, k_t, heads):
    B, T, C = c_kv.shape
    qk_w = C + LANES
    tq, small, levels = ATTN_Q_TOKENS, ATTN_K_SMALL, ATTN_K_LEVELS
    assert T % small == 0 and small % tq == 0 and tq & (tq - 1) == 0 and C % LANES == 0
    assert all(a % b == 0 for a, b in zip(levels, levels[1:]))
    rows = tq * heads
    kern = functools.partial(_attn_prompt_kernel, heads=heads, tq=tq, small=small, levels=levels, kv_lora=C,
                             sub_rows=math.gcd(rows, ATTN_SUB_ROWS))
    return pl.pallas_call(
        kern,
        grid=(B, T // tq),
        in_specs=[pl.BlockSpec((1, tq, heads * qk_w), lambda b, i: (b, i, 0)),
                  pl.BlockSpec((1, T, C), lambda b, i: (b, 0, 0)),
                  pl.BlockSpec((1, T // small, qk_w, small), lambda b, i: (b, 0, 0, 0))],
        out_specs=pl.BlockSpec((1, tq, heads * C), lambda b, i: (b, i, 0)),
        out_shape=jax.ShapeDtypeStruct((B, T, heads * C), BF16),
        scratch_shapes=[pltpu.VMEM((rows, qk_w), BF16),
                        pltpu.VMEM((rows, LANES), F32), pltpu.VMEM((rows, LANES), F32),
                        pltpu.VMEM((rows, C), F32)],
        compiler_params=_cparams("parallel", "arbitrary"),
        name="attn_prompt",
    )(q, c_kv, k_t)


def _attn_sample_kernel(pt_ref, ql_ref, qp_ref, cn_ref, knT_ref, cache_c, cache_kT, o_ref,
                        cbuf, kbuf, sem, m_scr, l_scr, acc_scr,
                        *, layer, group, n_groups, total, slots, parts, page, heads):
    b = pl.program_id(0)

    def copies(gid):
        slot = gid % slots
        out = []
        for g in range(group):
            pg = pt_ref[gid * group + g]
            out.append(pltpu.make_async_copy(cache_c.at[layer, pg], cbuf.at[slot, pl.ds(g * page, page), :],
                                             sem.at[0, slot]))
            out.append(pltpu.make_async_copy(cache_kT.at[layer, pg], kbuf.at[slot, :, pl.ds(g * page, page)],
                                             sem.at[1, slot]))
        return out

    @pl.when(b == 0)
    def _():
        for g0 in range(min(slots - 1, total)):
            for cp in copies(g0):
                cp.start()

    m_scr[...] = jnp.full(m_scr.shape, -jnp.inf, F32)
    l_scr[...] = jnp.zeros(l_scr.shape, F32)
    acc_scr[...] = jnp.zeros(acc_scr.shape, F32)
    ql = ql_ref[0]
    qp = qp_ref[0]

    def body(j, carry):
        gid = b * n_groups + j

        @pl.when(gid + slots - 1 < total)
        def _():
            for cp in copies(gid + slots - 1):
                cp.start()

        for cp in copies(gid):
            cp.wait()
        slot = gid % slots
        part = group * page // parts
        kcs = [cbuf[slot, i * part:(i + 1) * part, :].astype(BF16) for i in range(parts)]
        kTs = [kbuf[slot, :, i * part:(i + 1) * part].astype(BF16) for i in range(parts)]
        scores = [_dot_nt(ql, kcs[i]) + _dot(qp, kTs[i]) for i in range(parts)]
        for i in range(parts):
            _softmax_update(scores[i], kcs[i], m_scr, l_scr, acc_scr)
        return carry

    lax.fori_loop(0, n_groups, body, 0)

    cn = cn_ref[0]
    s = _dot_nt(ql, cn) + _dot(qp, knT_ref[0])
    rows = s.shape[0]
    tok = lax.broadcasted_iota(jnp.int32, (rows, LANES), 0) // heads
    key = lax.broadcasted_iota(jnp.int32, (rows, LANES), 1)
    _softmax_update(jnp.where(key <= tok, s, NEG_INF), cn, m_scr, l_scr, acc_scr)
    inv = 1.0 / l_scr[...]
    o_ref[0] = (acc_scr[...] * jnp.tile(inv, (1, acc_scr.shape[1] // LANES))).astype(o_ref.dtype)


def _attn_sample(q_lat, q_pe, c_new, k_new_t, cache_c, cache_kT, page_table, layer, heads):
    DB, rows, C = q_lat.shape
    R = cache_kT.shape[2]
    page = cache_c.shape[2]
    n_pages = page_table.shape[1]
    assert page == LANES and rows // heads <= LANES
    group = math.gcd(PAGES_PER_GROUP, n_pages)
    n_groups = n_pages // group
    pt = page_table.reshape(-1).astype(jnp.int32)
    per_b = lambda shape: pl.BlockSpec((1,) + shape, lambda b, pt_ref: (b, 0, 0))
    slots = SAMPLE_SLOTS
    kern = functools.partial(_attn_sample_kernel, layer=layer, group=group, n_groups=n_groups,
                             total=DB * n_groups, slots=slots, parts=math.gcd(SAMPLE_PARTS, group), page=page,
                             heads=heads)
    grid_spec = pltpu.PrefetchScalarGridSpec(
        num_scalar_prefetch=1,
        grid=(DB,),
        in_specs=[per_b((rows, C)), per_b((rows, R)), per_b((LANES, C)), per_b((R, LANES)),
                  pl.BlockSpec(memory_space=pl.ANY), pl.BlockSpec(memory_space=pl.ANY)],
        out_specs=per_b((rows, C)),
        scratch_shapes=[pltpu.VMEM((slots, group * page, C), F32), pltpu.VMEM((slots, R, group * page), F32),
                        pltpu.SemaphoreType.DMA((2, slots)),
                        pltpu.VMEM((rows, LANES), F32), pltpu.VMEM((rows, LANES), F32),
                        pltpu.VMEM((rows, C), F32)],
    )
    return pl.pallas_call(
        kern,
        grid_spec=grid_spec,
        out_shape=jax.ShapeDtypeStruct((DB, rows, C), BF16),
        compiler_params=_cparams("arbitrary"),
        name="attn_sample",
    )(pt, q_lat, q_pe, c_new, k_new_t, cache_c, cache_kT)


def _mix_ffn_kernel(x_ref, p_ref, q_ref, wo_ref, g_ref, wup_ref, wdn_ref, gnext_ref, *rest, mla_heads, emit_h):
    outs, (xn_scr, acc_scr, *cat_scr) = rest[:1 + emit_h], rest[1 + emit_h:]
    nnext_ref = outs[-1]
    k = pl.program_id(2)

    @pl.when(k == 0)
    def _():
        if mla_heads:
            kv_lora, vhead = q_ref.shape[1:]
            for h in range(mla_heads):
                oh = _dot(p_ref[:, h * kv_lora:(h + 1) * kv_lora], q_ref[h])
                cat_scr[0][:, h * vhead:(h + 1) * vhead] = oh.astype(BF16)
            mixed = cat_scr[0][...]
        else:
            mixed = (p_ref[...] * q_ref[...]).astype(BF16)
        h = x_ref[...] + _dot(mixed, wo_ref[...])
        xn_scr[...] = _rms(h, g_ref[...]).astype(BF16)
        acc_scr[...] = h

    u = jnp.maximum(_dot(xn_scr[...], wup_ref[...]), 0.0)
    acc_scr[...] += _dot((u * u).astype(BF16), wdn_ref[...])

    @pl.when(k == pl.num_programs(2) - 1)
    def _():
        out = acc_scr[...]
        if emit_h:
            outs[0][...] = out
        nnext_ref[...] = _rms(out, gnext_ref[...])


def _mix_ffn(x, p, q, wo, g, wup, wdn, g_next, *, mla_heads=0, emit_h=True, window=None):
    N, D = x.shape
    FF = wup.shape[1]
    stride, offset, length = (N, 0, N) if window is None else window
    n_seq = N // stride
    tf = min(FF_TILE, FF)
    if offset == 0 and length == stride:
        tm = _row_tile(length)
        rows = lambda w: pl.BlockSpec((tm, w), lambda b, i, k: (b * (stride // tm) + i, 0))
    else:
        tm = _row_tile(length, FFN_WINDOW_TILE)
        assert stride % 8 == 0 and offset % 8 == 0 and tm % 8 == 0
        rows = lambda w: pl.BlockSpec((pl.Element(tm), pl.Element(w)),
                                      lambda b, i, k: (pl.multiple_of(b * stride + offset + i * tm, 8), 0))
    const = lambda arr: pl.BlockSpec(arr.shape, lambda b, i, k: (0,) * arr.ndim)
    out_spec = pl.BlockSpec((tm, D), lambda b, i, k: (b * (length // tm) + i, 0))
    out_sds = jax.ShapeDtypeStruct((n_seq * length, D), F32)
    n_out = 2 if emit_h else 1
    scratch = [pltpu.VMEM((tm, D), BF16), pltpu.VMEM((tm, D), F32)]
    if mla_heads:
        scratch.append(pltpu.VMEM((tm, wo.shape[0]), BF16))
    res = pl.pallas_call(
        functools.partial(_mix_ffn_kernel, mla_heads=mla_heads, emit_h=emit_h),
        grid=(n_seq, length // tm, FF // tf),
        in_specs=[rows(D), rows(p.shape[1]), const(q) if mla_heads else rows(D), const(wo), const(g),
                  pl.BlockSpec((D, tf), lambda b, i, k: (0, k)),
                  pl.BlockSpec((tf, D), lambda b, i, k: (k, 0)),
                  const(g_next)],
        out_specs=[out_spec] * n_out,
        out_shape=[out_sds] * n_out,
        scratch_shapes=scratch,
        compiler_params=_cparams("parallel", "parallel", "arbitrary"),
        name="mix_ffn",
    )(x, p, q, wo, g, wup, wdn, g_next)
    return res if emit_h else (None, res[0])


def _rwkv_proj_kernel(n_ref, xp_ref, mu_ref, wr_ref, wk_ref, wv_ref, dw0_ref, dw1_ref, dw2_ref,
                      aw0_ref, aw1_ref, aw2_ref, gw1_ref, gw2_ref,
                      r_ref, k_ref, v_ref, d_ref, a_ref, g_ref, *, seq_len):
    n = n_ref[...]
    if seq_len is None:
        x_prev = xp_ref[...]
    else:
        tm = n.shape[0]
        inside = ((pl.program_id(0) * tm) % seq_len != 0).astype(F32)
        row = lax.broadcasted_iota(jnp.int32, (tm, 1), 0)
        x_prev = jnp.where(row == 0, xp_ref[7:8, :] * inside, pltpu.roll(n, 1, axis=0))
    xx = x_prev - n
    mix = lambda j: (n + xx * mu_ref[j:j + 1, :]).astype(BF16)
    r_ref[...] = _dot(mix(0), wr_ref[...])
    k_ref[...] = _dot(mix(2), wk_ref[...])
    v_ref[...] = _dot(mix(3), wv_ref[...])
    z = dw0_ref[...] + _dot(jnp.tanh(_dot(mix(1), dw1_ref[...])).astype(BF16), dw2_ref[...])
    d_ref[...] = (-math.exp(-0.5)) / (1.0 + jnp.exp(-z))
    za = aw0_ref[...] + _dot(_dot(mix(4), aw1_ref[...]).astype(BF16), aw2_ref[...])
    a_ref[...] = 1.0 / (1.0 + jnp.exp(-za))
    zg = _dot(mix(5), gw1_ref[...])
    g_ref[...] = _dot((1.0 / (1.0 + jnp.exp(-zg))).astype(BF16), gw2_ref[...])


def _rwkv_project(n, x_prev, wts, seq_len=None):
    N, D = n.shape
    tm = _row_tile(N if seq_len is None else seq_len, RWKV_ROW_TILE)
    row = pl.BlockSpec((tm, D), lambda i: (i, 0))
    full = lambda arr: pl.BlockSpec(arr.shape, lambda i: (0,) * arr.ndim)
    if seq_len is None:
        prev_spec = row
    else:
        x_prev = n
        prev_spec = pl.BlockSpec((8, D), lambda i: (jnp.maximum(i * (tm // 8) - 1, 0), 0))
    return pl.pallas_call(
        functools.partial(_rwkv_proj_kernel, seq_len=seq_len),
        grid=(N // tm,),
        in_specs=[row, prev_spec] + [full(w) for w in wts],
        out_specs=[row] * 6,
        out_shape=[jax.ShapeDtypeStruct((N, D), F32)] * 6,
        compiler_params=_cparams("parallel"),
        name="rwkv_project",
    )(n, x_prev, *wts)


def _wkv_kernel(r_ref, k_ref, v_ref, d_ref, a_ref, kk_ref, ka_ref, rk_ref, lnw_ref, lnb_ref, s0_ref,
                y_ref, sout_ref, st_scr, *, L, t_valid, nb, pairs, hd):
    c = pl.program_id(0)
    L2 = 2 * L
    W = 2 * hd
    P = range(nb * pairs)
    bi = [q // pairs for q in P]
    hp = [q % pairs for q in P]
    m0 = lax.broadcasted_iota(jnp.int32, (1, W), 1) < hd

    @pl.when(c == 0)
    def _():
        z = jnp.zeros((hd, hd), F32)
        for p in P:
            s_bd = jnp.concatenate([jnp.concatenate([s0_ref[bi[p], 2 * hp[p]], z], axis=1),
                                    jnp.concatenate([z, s0_ref[bi[p], 2 * hp[p] + 1]], axis=1)], axis=0)
            st_scr[p] = s_bd.T

    valid = (c * L + lax.broadcasted_iota(jnp.int32, (L, 1), 0)) < t_valid
    tri = (lax.broadcasted_iota(jnp.int32, (L, L), 0) >= lax.broadcasted_iota(jnp.int32, (L, L), 1))
    tri3 = jnp.tile(tri.astype(BF16), (1, 3))
    ones3 = jnp.ones((3 * L, W), BF16)
    i2 = lax.broadcasted_iota(jnp.int32, (L2, L2), 0)
    j2 = lax.broadcasted_iota(jnp.int32, (L2, L2), 1)
    mask_s = j2 < i2
    mask_i = j2 <= i2
    eye = (i2 == j2).astype(F32)

    def stack(x):
        return jnp.concatenate([jnp.where(m0, x, 0.0), jnp.where(m0, 0.0, x)], axis=0).astype(BF16)

    def head_sum(x):
        s_a = jnp.sum(jnp.where(m0, x, 0.0), axis=1, keepdims=True)
        s_b = jnp.sum(jnp.where(m0, 0.0, x), axis=1, keepdims=True)
        return jnp.where(m0, s_a, s_b)

    def split3(x):
        hi = x.astype(BF16)
        r1 = x - hi.astype(F32)
        mid = r1.astype(BF16)
        lo = (r1 - mid.astype(F32)).astype(BF16)
        return jnp.concatenate([hi, mid, lo], axis=0)

    sl = [slice(hp[p] * W, (hp[p] + 1) * W) for p in P]
    load = lambda ref, p: jnp.where(valid, ref[bi[p], :, sl[p]], 0.0)
    r = [load(r_ref, p) for p in P]
    k = [load(k_ref, p) for p in P]
    v = [load(v_ref, p) for p in P]
    d = [load(d_ref, p) for p in P]
    a = [load(a_ref, p) for p in P]
    d3 = [split3(d[p]) for p in P]
    cum = [_dot(tri3, d3[p]) for p in P]
    kk = [k[p] * kk_ref[:, sl[p]] for p in P]
    kk = [kk[p] * lax.rsqrt(jnp.maximum(head_sum(kk[p] * kk[p]), KK_NORM_EPS ** 2)) for p in P]
    kp = [k[p] * (1.0 + (a[p] - 1.0) * ka_ref[:, sl[p]]) for p in P]
    bv = [kk[p] * a[p] for p in P]
    e_neg = [jnp.exp(-cum[p]) for p in P]
    a_st = [stack(-kk[p] * jnp.exp(cum[p] - d[p])) for p in P]
    r_st = [stack(r[p] * jnp.exp(cum[p])) for p in P]
    b_st = [stack(bv[p] * e_neg[p]) for p in P]
    k_st = [stack(kp[p] * e_neg[p]) for p in P]
    v_st = [stack(v[p]) for p in P]
    pm = [_dot_nt(jnp.concatenate([a_st[p], r_st[p]], axis=0), jnp.concatenate([b_st[p], k_st[p]], axis=0))
          for p in P]
    m_ab = [jnp.where(mask_s, pm[p][:L2, :L2], 0.0) for p in P]
    m_ak = [jnp.where(mask_s, pm[p][:L2, L2:], 0.0).astype(BF16) for p in P]
    m_rb = [jnp.where(mask_i, pm[p][L2:, :L2], 0.0).astype(BF16) for p in P]
    m_rk = [jnp.where(mask_i, pm[p][L2:, L2:], 0.0).astype(BF16) for p in P]
    st = [st_scr[p] for p in P]
    stb = [st[p].astype(BF16) for p in P]
    rhs = [_dot(jnp.concatenate([a_st[p], m_ak[p]], axis=1), jnp.concatenate([stb[p], v_st[p]], axis=0)) for p in P]
    inv = [eye + m_ab[p] for p in P]
    pw = [m_ab[p].astype(BF16) for p in P]
    pw = [_dot(pw[p], pw[p]).astype(BF16) for p in P]
    levels = int(math.log2(L))
    for lvl in range(1, levels):
        last = lvl == levels - 1
        z = [_dot(pw[p], inv[p].astype(BF16) if last else jnp.concatenate([inv[p].astype(BF16), pw[p]], axis=1))
             for p in P]
        inv = [inv[p] + z[p][:, :L2] for p in P]
        if not last:
            pw = [z[p][:, L2:].astype(BF16) for p in P]
    u_st = [_dot(inv[p].astype(BF16), rhs[p].astype(BF16)).astype(BF16) for p in P]
    y_st = [_dot(jnp.concatenate([r_st[p], m_rb[p], m_rk[p]], axis=1),
                 jnp.concatenate([stb[p], u_st[p], v_st[p]], axis=0)) for p in P]
    decay_col = [jnp.exp(_dot_tn(d3[p], ones3)) for p in P]
    e_rest = [jnp.exp(cum[p][L - 1:L, :] - cum[p]) for p in P]
    for p in P:
        st_scr[p] = decay_col[p] * st[p] + _dot_tn(
            jnp.concatenate([stack(bv[p] * e_rest[p]), stack(kp[p] * e_rest[p])], axis=0),
            jnp.concatenate([u_st[p], v_st[p]], axis=0))
    for p in P:
        y = y_st[p][:L] + y_st[p][L:]
        mean = head_sum(y) * (1.0 / hd)
        yc = y - mean
        var = head_sum(yc * yc) * (1.0 / hd)
        yn = yc * lax.rsqrt(var + GN_EPS) * lnw_ref[:, sl[p]] + lnb_ref[:, sl[p]]
        y_ref[bi[p], :, sl[p]] = yn + head_sum(r[p] * kp[p] * rk_ref[:, sl[p]]) * v[p]

    @pl.when(c == pl.num_programs(0) - 1)
    def _():
        for p in P:
            s_bd = st_scr[p].T
            sout_ref[bi[p], 2 * hp[p]] = s_bd[:hd, :hd]
            sout_ref[bi[p], 2 * hp[p] + 1] = s_bd[hd:, hd:]


def _wkv(r, k, v, d, a, params, s0, L, t_valid):
    B, T, D = r.shape
    H, hd = s0.shape[1], s0.shape[2]
    assert 2 * hd == LANES and H % 2 == 0 and T % L == 0 and (2 * L) % LANES == 0
    pairs = H // 2
    seq = pl.BlockSpec((B, L, D), lambda c: (0, c, 0))
    vec = pl.BlockSpec((1, D), lambda c: (0, 0))
    state = pl.BlockSpec((B, H, hd, hd), lambda c: (0, 0, 0, 0))
    kern = functools.partial(_wkv_kernel, L=L, t_valid=t_valid, nb=B, pairs=pairs, hd=hd)
    return pl.pallas_call(
        kern,
        grid=(T // L,),
        in_specs=[seq] * 5 + [vec] * 5 + [state],
        out_specs=[seq, state],
        out_shape=[jax.ShapeDtypeStruct((B, T, D), F32), jax.ShapeDtypeStruct(s0.shape, F32)],
        scratch_shapes=[pltpu.VMEM((B * pairs, LANES, LANES), F32)],
        compiler_params=_cparams("arbitrary"),
        name="wkv",
    )(r, k, v, d, a, *params, s0)


def _wkv_sample_kernel(r_ref, k_ref, v_ref, d_ref, a_ref, kk_ref, ka_ref, rk_ref, lnw_ref, lnb_ref, s0_ref,
                       y_ref, sout_ref, w_scr, a_scr, b_scr, kp_scr, yraw_scr, *, steps, hd):
    for t in range(steps):
        kt = k_ref[t]
        at = a_ref[t]
        kk = kt * kk_ref[...]
        kk = kk * lax.rsqrt(jnp.maximum(jnp.sum(kk * kk, axis=0, keepdims=True), KK_NORM_EPS ** 2))
        w_scr[t] = jnp.exp(d_ref[t])
        a_scr[t] = -kk
        b_scr[t] = kk * at
        kp_scr[t] = kt * (1.0 + (at - 1.0) * ka_ref[...])

    def body(i, carry):
        rows = [i * WKV_SAMPLE_ROWS + j for j in range(WKV_SAMPLE_ROWS)]
        s = [s0_ref[vi] for vi in rows]
        for t in range(steps):
            sa = [jnp.sum(s[j] * a_scr[t], axis=0, keepdims=True) for j in range(len(rows))]
            s = [s[j] * w_scr[t] + sa[j] * b_scr[t] + v_ref[t, pl.ds(vi, 1), :] * kp_scr[t]
                 for j, vi in enumerate(rows)]
            for j, vi in enumerate(rows):
                yraw_scr[t, pl.ds(vi, 1), :] = jnp.sum(s[j] * r_ref[t], axis=0, keepdims=True)
        for j, vi in enumerate(rows):
            sout_ref[vi] = s[j]
        return carry

    assert hd % WKV_SAMPLE_ROWS == 0
    lax.fori_loop(0, hd // WKV_SAMPLE_ROWS, body, 0)
    for t in range(steps):
        y = yraw_scr[t]
        mean = jnp.mean(y, axis=0, keepdims=True)
        yc = y - mean
        var = jnp.mean(yc * yc, axis=0, keepdims=True)
        yn = yc * lax.rsqrt(var + GN_EPS) * lnw_ref[...] + lnb_ref[...]
        bonus = jnp.sum(r_ref[t] * kp_scr[t] * rk_ref[...], axis=0, keepdims=True)
        y_ref[t] = yn + bonus * v_ref[t]


def _wkv_sample(r, k, v, d, a, params, s0):
    S, D, DB = r.shape
    H, hd = s0.shape[0], s0.shape[1]
    seq = pl.BlockSpec((S, hd, DB), lambda h: (0, h, 0))
    vec = pl.BlockSpec((hd, DB), lambda h: (h, 0))
    state = pl.BlockSpec((None, hd, hd, DB), lambda h: (h, 0, 0, 0))
    kern = functools.partial(_wkv_sample_kernel, steps=S, hd=hd)
    return pl.pallas_call(
        kern,
        grid=(H,),
        in_specs=[seq] * 5 + [vec] * 5 + [state],
        out_specs=[seq, state],
        out_shape=[jax.ShapeDtypeStruct((S, D, DB), F32), jax.ShapeDtypeStruct(s0.shape, F32)],
        scratch_shapes=[pltpu.VMEM((S, hd, DB), F32)] * 5,
        compiler_params=_cparams("parallel"),
        name="wkv_sample",
    )(r, k, v, d, a, *params, s0)


def _rope_tables(pos, rope, heads):
    half = rope // 2
    inv_freq = ROPE_THETA ** (-jnp.arange(half, dtype=F32) / half)
    ang = pos.astype(F32)[:, None] * inv_freq[None, :]
    cos, sin = jnp.cos(ang), jnp.sin(ang)
    cos2 = jnp.concatenate([cos, cos], axis=-1)
    sin2 = jnp.concatenate([-sin, sin], axis=-1)
    pad = lambda t: jnp.pad(t, ((0, 0), (0, LANES - rope)))
    return pad(cos2), pad(sin2), cos2.T, sin2.T


def _swap_halves(w, width):
    lead = w.shape[:-1]
    g = w.reshape(lead + (-1, 2, width // 2))
    return g[..., ::-1, :].reshape(w.shape)


def _pad_heads(w, heads, rope):
    K = w.shape[0]
    return jnp.pad(w.reshape(K, heads, rope), ((0, 0), (0, 0), (0, LANES - rope))).reshape(K, heads * LANES)


def kernel(x_prompt, x_sample, cache_kv_latent, cache_k_rope, state_wkv, state_shift, page_table, meta_tokens, norm_mix, norm_ffn, norm_final, mla_w_qkv_a, mla_q_a_norm, mla_kv_a_norm, mla_w_q_b, mla_w_kv_b, mla_w_o, rwkv_mu, rwkv_w_r, rwkv_w_k, rwkv_w_v, rwkv_w_o, rwkv_decay_w0, rwkv_decay_w1, rwkv_decay_w2, rwkv_a_w0, rwkv_a_w1, rwkv_a_w2, rwkv_g_w1, rwkv_g_w2, rwkv_k_k, rwkv_k_a, rwkv_r_k, rwkv_ln_w, rwkv_ln_b, ffn_w_up, ffn_w_down):
    B, SEQ, D = x_prompt.shape
    DB, S, _ = x_sample.shape
    n_meta = meta_tokens.shape[0]
    T = n_meta + SEQ
    Tp = -(-T // SEQ_ALIGN) * SEQ_ALIGN
    page = cache_kv_latent.shape[2]
    past_len = page_table.shape[1] * page
    kv_lora = cache_kv_latent.shape[-1]
    rope = cache_k_rope.shape[-1]
    q_lora = mla_q_a_norm.shape[-1]
    H, hd = state_wkv.shape[2], state_wkv.shape[3]
    qk = mla_w_q_b.shape[-1]
    kvb = mla_w_kv_b.shape[-1]
    ov = mla_w_o.shape[1]
    heads = (qk + ov - kvb) // rope
    nope = qk // heads - rope
    vhead = ov // heads
    qk_w = kv_lora + LANES
    scale = float(nope + rope) ** -0.5 * LOG2E
    dims = (q_lora, kv_lora, rope, heads, nope, scale)

    row = lambda vec: vec.reshape(1, -1).astype(F32)
    b16 = lambda w: w.astype(BF16)

    meta = jnp.broadcast_to(meta_tokens.astype(F32)[None], (B, n_meta, D))
    hp = jnp.concatenate([meta, x_prompt, jnp.zeros((B, Tp - T, D), F32)], axis=1)
    hs = x_sample.reshape(1, DB * S, D)

    l = 0
    wqkv = mla_w_qkv_a[l]
    wa = b16(wqkv[:, :q_lora + kv_lora])
    wkvT = b16(wqkv[:, q_lora:q_lora + kv_lora].T)
    wk = wqkv[:, q_lora + kv_lora:]
    wkT = b16(jnp.concatenate([wk, _swap_halves(wk, rope)], axis=1).T)
    wqb = mla_w_q_b[l].reshape(q_lora, heads, nope + rope)
    wnope = b16(wqb[..., :nope].reshape(q_lora, heads * nope))
    wpe = wqb[..., nope:].reshape(q_lora, heads * rope)
    wpesw = b16(_pad_heads(_swap_halves(wpe, rope), heads, rope))
    wpe = b16(_pad_heads(wpe, heads, rope))
    wkvb = mla_w_kv_b[l].reshape(kv_lora, heads, nope + vhead)
    wuk = b16(jnp.transpose(wkvb[..., :nope], (1, 2, 0)))
    wuv = b16(jnp.transpose(wkvb[..., nope:], (1, 0, 2)))
    wo = b16(mla_w_o[l])
    kvn = mla_kv_a_norm[l].astype(F32)
    mla_w = (wa, wkvT, wkT, row(mla_q_a_norm[l]), row(kvn), kvn.reshape(-1, 1), wnope, wpe, wpesw, wuk)

    tabs_p = _rope_tables(jnp.arange(Tp), rope, heads)
    tabs_s = _rope_tables(jnp.tile(past_len + jnp.arange(S), DB), rope, heads)

    q_p, c_p, kpeT_p, cb_p, kT_p = _mla_project(hp, row(norm_mix[0]), mla_w, tabs_p, dims)
    q_s, c_s, kpeT_s, cb_s, kT_s = _mla_project(hs, row(norm_mix[0]), mla_w, tabs_s, dims)

    ks = ATTN_K_SMALL
    kT_chunks = kT_p.reshape(B, qk_w, Tp // ks, ks).transpose(0, 2, 1, 3)
    o_p = _attn_prompt(q_p, cb_p, kT_chunks, heads)

    q_s = q_s.reshape(DB, S * heads, qk_w)
    c_new = jnp.pad(cb_s.reshape(DB, S, kv_lora), ((0, 0), (0, LANES - S), (0, 0)))
    k_new_t = kT_s[0, kv_lora:kv_lora + rope].reshape(rope, DB, S).transpose(1, 0, 2)
    k_new_t = jnp.pad(k_new_t, ((0, 0), (0, 0), (0, LANES - S)))
    o_s = _attn_sample(q_s[..., :kv_lora], q_s[..., kv_lora:kv_lora + rope], c_new, k_new_t,
                       cache_kv_latent, jnp.swapaxes(cache_k_rope, 2, 3), page_table, l, heads)

    wup0, wdn0 = b16(ffn_w_up[0]), b16(ffn_w_down[0])
    hp, np_ = _mix_ffn(hp.reshape(B * Tp, D), o_p.reshape(B * Tp, heads * kv_lora), wuv, wo,
                       row(norm_ffn[0]), wup0, wdn0, row(norm_mix[1]), mla_heads=heads)
    hs, ns_ = _mix_ffn(hs.reshape(DB * S, D), o_s.reshape(DB * S, heads * kv_lora), wuv, wo,
                       row(norm_ffn[0]), wup0, wdn0, row(norm_mix[1]), mla_heads=heads)

    np3 = np_.reshape(B, Tp, D)
    ns3 = ns_.reshape(DB, S, D)
    xprev_s = jnp.concatenate([state_shift[l].astype(F32)[:, None], ns3[:, :-1]], axis=1).reshape(DB * S, D)
    rw = (rwkv_mu[l].astype(F32), b16(rwkv_w_r[l]), b16(rwkv_w_k[l]), b16(rwkv_w_v[l]),
          row(rwkv_decay_w0[l]), b16(rwkv_decay_w1[l]), b16(rwkv_decay_w2[l]),
          row(rwkv_a_w0[l]), b16(rwkv_a_w1[l]), b16(rwkv_a_w2[l]),
          b16(rwkv_g_w1[l]), b16(rwkv_g_w2[l]))
    par = (rwkv_k_k[l], rwkv_k_a[l], rwkv_r_k[l], rwkv_ln_w[l], rwkv_ln_b[l])

    r_p, k_p, v_p, d_p, a_p, g_p = _rwkv_project(np_, None, rw, seq_len=Tp)
    r_s, k_s, v_s, d_s, a_s, g_s = _rwkv_project(ns_, xprev_s, rw)

    seq_p = lambda t: t.reshape(B, Tp, D)
    y_p, st_p = _wkv(seq_p(r_p), seq_p(k_p), seq_p(v_p), seq_p(d_p), seq_p(a_p), tuple(row(x) for x in par),
                     jnp.zeros((B, H, hd, hd), F32), WKV_CHUNK, T)

    lanes_b = lambda t: t.reshape(DB, S, D).transpose(1, 2, 0)
    par_b = tuple(jnp.broadcast_to(x.reshape(D, 1).astype(F32), (D, DB)) for x in par)
    y_s, st_s = _wkv_sample(lanes_b(r_s), lanes_b(k_s), lanes_b(v_s), lanes_b(d_s), lanes_b(a_s), par_b,
                            jnp.transpose(state_wkv[l].astype(F32), (1, 2, 3, 0)))
    y_s = y_s.transpose(2, 0, 1).reshape(DB * S, D)
    st_s = jnp.transpose(st_s, (3, 0, 1, 2))

    wo_r = b16(rwkv_w_o[l])
    wup1, wdn1 = b16(ffn_w_up[1]), b16(ffn_w_down[1])
    _, yp = _mix_ffn(hp, y_p.reshape(B * Tp, D), g_p, wo_r, row(norm_ffn[1]), wup1, wdn1, row(norm_final),
                     emit_h=False, window=(Tp, n_meta, SEQ))
    _, ys = _mix_ffn(hs, y_s, g_s, wo_r, row(norm_ffn[1]), wup1, wdn1, row(norm_final), emit_h=False)

    y_prompt = yp.reshape(B, SEQ, D)
    y_sample = ys.reshape(DB, S, D)
    k_rope_p = jnp.swapaxes(kpeT_p[:, :, :T], 1, 2)
    k_rope_s = kpeT_s[0].reshape(rope, DB, S).transpose(1, 2, 0)
    return (y_prompt, y_sample,
            c_p[None, :, :T], k_rope_p[None],
            c_s.reshape(1, DB, S, kv_lora), k_rope_s[None],
            st_p[None], np3[None, :, T - 1], st_s[None], ns3[None, :, S - 1])
```
